```python
import numpy as np
import jax, jax.numpy as jnp
from jax import lax

D_MODEL = 1024
BATCH = 4
SEQ = 4096
DEPTH = 2

RET_HEADS = 4
RET_DV = D_MODEL // RET_HEADS
RET_DK = RET_DV // 2
RET_CHUNK = 128
SC_WIDTH = D_MODEL
CONV_WIDTH = 3
NSA_DH = 64
NSA_HEADS = D_MODEL // NSA_DH
NSA_KV_HEADS = 4
NSA_HPG = NSA_HEADS // NSA_KV_HEADS
CMP_BLOCK = 32
CMP_STRIDE = 16
CMP_HIDDEN = 256
SLC_BLOCK = 64
N_SELECT = 16
WINDOW = 512
Q_CHUNK = 64
D_FF = ((8 * D_MODEL // 3 + 127) // 128) * 128
N_BRANCH = 3
EPS = 1e-6
NEG = -1e30
FORCE = 1e6

RET_QK_W = RET_HEADS * RET_DK
RET_V_W = RET_HEADS * RET_DV
NSA_Q_W = NSA_HEADS * NSA_DH
NSA_KV_W = NSA_KV_HEADS * NSA_DH
IN_SPLITS = (RET_QK_W, RET_QK_W, RET_V_W, RET_V_W, SC_WIDTH, SC_WIDTH, SC_WIDTH,
             NSA_Q_W, 6 * NSA_KV_W, 3 * NSA_HEADS, N_BRANCH * D_MODEL)
IN_WIDTH = sum(IN_SPLITS)
IN_OFFSETS = tuple(np.cumsum(IN_SPLITS)[:-1].tolist())

kernel_name = 'hybrid_retention_shortconv_nsa_block'


def rms_norm(x, w):
    xf = x.astype(jnp.float32)
    y = xf * lax.rsqrt(jnp.mean(xf * xf, axis=-1, keepdims=True) + EPS)
    return (y * w.astype(jnp.float32)).astype(x.dtype)


def causal_dwconv(x, w):
    c = x.shape[-1]
    return lax.conv_general_dilated(x, w[:, None, :].astype(x.dtype), window_strides=(1,),
                                    padding=((CONV_WIDTH - 1, 0),),
                                    dimension_numbers=('NWC', 'WIO', 'NWC'),
                                    feature_group_count=c)


def rotary(x, cos, sin):
    x1, x2 = jnp.split(x, 2, axis=-1)
    return jnp.concatenate([x1 * cos - x2 * sin, x1 * sin + x2 * cos], axis=-1)


def retention(q, k, v, g, norm_w):
    b, s = q.shape[:2]
    dt = q.dtype
    pos = jnp.arange(s, dtype=jnp.float32)
    theta = 10000.0 ** (-jnp.linspace(0.0, 1.0, RET_DK // 2, dtype=jnp.float32))
    ang = pos[:, None] * theta[None, :]
    cos = jnp.cos(ang)[None, :, None, :].astype(dt)
    sin = jnp.sin(ang)[None, :, None, :].astype(dt)
    q = rotary(q, cos, sin)
    k = rotary(k, cos, sin) * (RET_DK ** -0.5)
    log_gamma = jnp.log1p(-(2.0 ** (-5.0 - jnp.arange(RET_HEADS, dtype=jnp.float32))))
    c = RET_CHUNK
    n = s // c
    qc = q.reshape(b, n, c, RET_HEADS, RET_DK)
    kc = k.reshape(b, n, c, RET_HEADS, RET_DK)
    vc = v.reshape(b, n, c, RET_HEADS, RET_DV)
    j = jnp.arange(c, dtype=jnp.float32)
    rel = j[:, None] - j[None, :]
    dmask = jnp.where(rel >= 0, jnp.exp(log_gamma[:, None, None] * jnp.maximum(rel, 0.0)), 0.0).astype(dt)
    scores = jnp.einsum('bnqhd,bnkhd->bnhqk', qc, kc) * dmask
    o_intra = jnp.einsum('bnhqk,bnkhv->bnqhv', scores, vc)
    zeta = jnp.exp(log_gamma[:, None] * (c - 1 - j)[None, :]).astype(dt)
    kv = jnp.einsum('bnchd,bnchv,hc->nbhdv', kc, vc, zeta)
    chunk_decay = jnp.exp(log_gamma * c).astype(dt)[None, :, None, None]

    def step(state, kv_i):
        return state * chunk_decay + kv_i, state

    _, prev = lax.scan(step, jnp.zeros((b, RET_HEADS, RET_DK, RET_DV), dt), kv)
    xi = jnp.exp(log_gamma[:, None] * (j + 1.0)[None, :]).astype(dt)
    o_cross = jnp.einsum('bnchd,nbhdv,hc->bnchv', qc, prev, xi)
    o = (o_intra + o_cross).reshape(b, s, RET_HEADS, RET_DV).astype(jnp.float32)
    mu = jnp.mean(o, axis=-1, keepdims=True)
    var = jnp.mean(jnp.square(o - mu), axis=-1, keepdims=True)
    o = ((o - mu) * lax.rsqrt(var + EPS)).reshape(b, s, RET_V_W) * norm_w.astype(jnp.float32)
    return (jax.nn.silu(g.astype(jnp.float32)) * o).astype(dt)


def compress(kv, pos, w1, w2):
    b, s = kv.shape[:2]
    nc = (s - CMP_BLOCK) // CMP_STRIDE + 1
    idx = np.arange(nc)[:, None] * CMP_STRIDE + np.arange(CMP_BLOCK)[None, :]
    blk = kv[:, idx] + pos[None, None, :, None, :]
    blk = blk.transpose(0, 1, 3, 2, 4).reshape(b, nc, NSA_KV_HEADS, CMP_BLOCK * NSA_DH)
    return jax.nn.gelu(blk @ w1) @ w2


def nsa_attention(q, kc_raw, vc_raw, ks, vs, kw, vw, gates, cmp_pos, cmp_w1, cmp_w2):
    b, s = q.shape[:2]
    dt = q.dtype
    scale = NSA_DH ** -0.5
    kc = compress(kc_raw, cmp_pos[0], cmp_w1[0], cmp_w2[0])
    vc = compress(vc_raw, cmp_pos[1], cmp_w1[1], cmp_w2[1])
    nc = kc.shape[1]
    ns = s // SLC_BLOCK
    n_sel = min(N_SELECT, ns)
    ci = np.arange(nc) * CMP_STRIDE
    sj = np.arange(ns) * SLC_BLOCK
    overlap = jnp.asarray(((ci[:, None] < sj[None, :] + SLC_BLOCK) &
                           (ci[:, None] + CMP_BLOCK > sj[None, :])).astype(np.float32))
    cmp_end = jnp.asarray(ci + CMP_BLOCK - 1)
    ks_blk = ks.reshape(b, ns, SLC_BLOCK, NSA_KV_HEADS, NSA_DH).transpose(0, 3, 1, 2, 4)
    vs_blk = vs.reshape(b, ns, SLC_BLOCK, NSA_KV_HEADS, NSA_DH).transpose(0, 3, 1, 2, 4)
    kw_pad = jnp.pad(kw, ((0, 0), (WINDOW, 0), (0, 0), (0, 0)))
    vw_pad = jnp.pad(vw, ((0, 0), (WINDOW, 0), (0, 0), (0, 0)))
    qg = q.reshape(b, s, NSA_KV_HEADS, NSA_HPG, NSA_DH)
    bi = jnp.arange(b)[:, None, None, None]
    gi = jnp.arange(NSA_KV_HEADS)[None, :, None, None]
    blk_ids = jnp.arange(ns)

    def chunk(cidx):
        t0 = cidx * Q_CHUNK
        t = t0 + jnp.arange(Q_CHUNK)
        qc = lax.dynamic_slice_in_dim(qg, t0, Q_CHUNK, axis=1)
        s1 = jnp.einsum('bqghd,bcgd->bghqc', qc, kc).astype(jnp.float32) * scale
        valid1 = cmp_end[None, :] <= t[:, None]
        any1 = (t >= CMP_BLOCK - 1).astype(jnp.float32)[:, None]
        p1 = jax.nn.softmax(jnp.where(valid1, s1, NEG), axis=-1) * any1
        o_cmp = jnp.einsum('bghqc,bcgd->bqghd', p1.astype(dt), vc)
        imp = jnp.einsum('bghqc,cs->bgqs', p1, overlap)
        cur = t // SLC_BLOCK
        causal_blk = blk_ids[None, :] * SLC_BLOCK <= t[:, None]
        forced = (blk_ids[None, :] == 0) | (blk_ids[None, :] == cur[:, None]) | (blk_ids[None, :] == cur[:, None] - 1)
        imp = jnp.where(forced & causal_blk, FORCE, imp)
        imp = jnp.where(causal_blk, imp, NEG)
        _, idx = lax.top_k(imp, n_sel)
        kg = ks_blk[bi, gi, idx]
        vg = vs_blk[bi, gi, idx]
        s2 = jnp.einsum('bqghd,bgqnld->bghqnl', qc, kg).astype(jnp.float32) * scale
        kpos2 = idx[..., None] * SLC_BLOCK + jnp.arange(SLC_BLOCK)
        mask2 = kpos2 <= t[None, None, :, None, None]
        s2 = jnp.where(mask2[:, :, None], s2, NEG)
        p2 = jax.nn.softmax(s2.reshape(s2.shape[:4] + (-1,)), axis=-1).reshape(s2.shape)
        o_slc = jnp.einsum('bghqnl,bgqnld->bqghd', p2.astype(dt), vg)
        kwc = lax.dynamic_slice_in_dim(kw_pad, t0, Q_CHUNK + WINDOW, axis=1)
        vwc = lax.dynamic_slice_in_dim(vw_pad, t0, Q_CHUNK + WINDOW, axis=1)
        kpos3 = t0 - WINDOW + jnp.arange(Q_CHUNK + WINDOW)
        mask3 = (kpos3[None, :] <= t[:, None]) & (kpos3[None, :] > t[:, None] - WINDOW) & (kpos3[None, :] >= 0)
        s3 = jnp.einsum('bqghd,brgd->bghqr', qc, kwc).astype(jnp.float32) * scale
        p3 = jax.nn.softmax(jnp.where(mask3, s3, NEG), axis=-1)
        o_win = jnp.einsum('bghqr,brgd->bqghd', p3.astype(dt), vwc)
        return o_cmp, o_slc, o_win

    o_cmp, o_slc, o_win = lax.map(chunk, jnp.arange(s // Q_CHUNK))

    def unchunk(o):
        return o.transpose(1, 0, 2, 3, 4, 5).reshape(b, s, NSA_HEADS, NSA_DH)

    o = (gates[..., 0:1] * unchunk(o_cmp) + gates[..., 1:2] * unchunk(o_slc)
         + gates[..., 2:3] * unchunk(o_win))
    return o.reshape(b, s, NSA_Q_W)


def token_mixer(h, w_in, ret_norm_w, w_ret_out, sc_conv_w, w_sc_out,
                cmp_pos, cmp_w1, cmp_w2, w_nsa_out, w_mix_out):
    b, s, _ = h.shape
    z = h @ w_in
    rq, rk, rv, rg, sb, sc, sx, nq, nkv, ng, mg = jnp.split(z, IN_OFFSETS, axis=-1)
    y_ret = retention(rq.reshape(b, s, RET_HEADS, RET_DK), rk.reshape(b, s, RET_HEADS, RET_DK),
                      rv.reshape(b, s, RET_HEADS, RET_DV), rg, ret_norm_w)
    y_sc = sb * causal_dwconv(sc * sx, sc_conv_w)
    k_c, v_c, k_s, v_s, k_w, v_w = jnp.split(nkv.reshape(b, s, 6 * NSA_KV_HEADS, NSA_DH), 6, axis=2)
    y_nsa = nsa_attention(nq.reshape(b, s, NSA_HEADS, NSA_DH), k_c, v_c, k_s, v_s, k_w, v_w,
                          jax.nn.sigmoid(ng).reshape(b, s, NSA_HEADS, 3), cmp_pos, cmp_w1, cmp_w2)
    g_ret, g_sc, g_nsa = jnp.split(jax.nn.sigmoid(mg), N_BRANCH, axis=-1)
    merged = g_ret * (y_ret @ w_ret_out) + g_sc * (y_sc @ w_sc_out) + g_nsa * (y_nsa @ w_nsa_out)
    return merged @ w_mix_out


def conv_glu_ffn(h, w_up, conv_w, w_down):
    u = causal_dwconv(h @ w_up, conv_w)
    a, v = jnp.split(u, 2, axis=-1)
    return (jax.nn.silu(a) * v) @ w_down


def setup_inputs(seed: int = 0) -> dict:
    key = jax.random.key(seed)
    ks = jax.random.split(key, 17)
    nrm = jax.random.normal
    f32 = jnp.float32
    return {
        'x': nrm(ks[0], (BATCH, SEQ, D_MODEL), f32),
        'attn_norm_w': 1.0 + 0.02 * nrm(ks[1], (DEPTH, D_MODEL), f32),
        'w_in': nrm(ks[2], (DEPTH, D_MODEL, IN_WIDTH), f32) * D_MODEL ** -0.5,
        'ret_norm_w': 1.0 + 0.02 * nrm(ks[3], (DEPTH, RET_V_W), f32),
        'w_ret_out': nrm(ks[4], (DEPTH, RET_V_W, D_MODEL), f32) * RET_V_W ** -0.5,
        'sc_conv_w': nrm(ks[5], (DEPTH, CONV_WIDTH, SC_WIDTH), f32) * CONV_WIDTH ** -0.5,
        'w_sc_out': nrm(ks[6], (DEPTH, SC_WIDTH, D_MODEL), f32) * SC_WIDTH ** -0.5,
        'nsa_cmp_pos': 0.02 * nrm(ks[7], (DEPTH, 2, CMP_BLOCK, NSA_DH), f32),
        'nsa_cmp_w1': nrm(ks[8], (DEPTH, 2, CMP_BLOCK * NSA_DH, CMP_HIDDEN), f32) * (CMP_BLOCK * NSA_DH) ** -0.5,
        'nsa_cmp_w2': nrm(ks[9], (DEPTH, 2, CMP_HIDDEN, NSA_DH), f32) * CMP_HIDDEN ** -0.5,
        'w_nsa_out': nrm(ks[10], (DEPTH, NSA_Q_W, D_MODEL), f32) * NSA_Q_W ** -0.5,
        'w_mix_out': nrm(ks[11], (DEPTH, D_MODEL, D_MODEL), f32) * D_MODEL ** -0.5,
        'ffn_norm_w': 1.0 + 0.02 * nrm(ks[12], (DEPTH, D_MODEL), f32),
        'w_ffn_up': nrm(ks[13], (DEPTH, D_MODEL, 2 * D_FF), f32) * D_MODEL ** -0.5,
        'ffn_conv_w': nrm(ks[14], (DEPTH, CONV_WIDTH, 2 * D_FF), f32) * CONV_WIDTH ** -0.5,
        'w_ffn_down': nrm(ks[15], (DEPTH, D_FF, D_MODEL), f32) * D_FF ** -0.5,
        'final_norm_w': 1.0 + 0.02 * nrm(ks[16], (D_MODEL,), f32),
    }


def reference(x, attn_norm_w, w_in, ret_norm_w, w_ret_out, sc_conv_w, w_sc_out,
              nsa_cmp_pos, nsa_cmp_w1, nsa_cmp_w2, w_nsa_out, w_mix_out,
              ffn_norm_w, w_ffn_up, ffn_conv_w, w_ffn_down, final_norm_w):
    for l in range(DEPTH):
        h = rms_norm(x, attn_norm_w[l])
        x = x + token_mixer(h, w_in[l], ret_norm_w[l], w_ret_out[l], sc_conv_w[l], w_sc_out[l],
                            nsa_cmp_pos[l], nsa_cmp_w1[l], nsa_cmp_w2[l], w_nsa_out[l], w_mix_out[l])
        h = rms_norm(x, ffn_norm_w[l])
        x = x + conv_glu_ffn(h, w_ffn_up[l], ffn_conv_w[l], w_ffn_down[l])
    return rms_norm(x, final_norm_w)
```

```python
import functools

import numpy as np
import jax
import jax.numpy as jnp
from jax import lax
from jax.experimental import pallas as pl
from jax.experimental.pallas import tpu as pltpu

F32 = jnp.float32
BF16 = jnp.bfloat16

D_MODEL = 1024
RET_HEADS = 4
RET_DV = D_MODEL // RET_HEADS
RET_DK = RET_DV // 2
RET_CHUNK = 128
CONV_WIDTH = 3
NSA_DH = 64
NSA_HEADS = D_MODEL // NSA_DH
NSA_KV_HEADS = 4
NSA_HPG = NSA_HEADS // NSA_KV_HEADS
CMP_BLOCK = 32
CMP_STRIDE = 16
CMP_HIDDEN = 256
SLC_BLOCK = 64
N_SELECT = 16
WINDOW = 512
D_FF = ((8 * D_MODEL // 3 + 127) // 128) * 128
EPS = 1e-6
NEG = -1e30
FORCE = 1e6

LANES = 128
HALO_ROWS = 8

OFF_RQ = 0
OFF_RK = 512
OFF_RV = 1024
OFF_RG = 2048
OFF_SB = 3072
OFF_SC = 4096
OFF_SX = 5120
OFF_NQ = 6144
OFF_MG = 7168
OFF_NKV = 10240
OFF_NG = 11776
Z_WIDTH = 12288

VMEM_LIMIT = 56 * 1024 * 1024


def _cparams(sem):
    return pltpu.CompilerParams(dimension_semantics=sem, vmem_limit_bytes=VMEM_LIMIT)


def _nt_dot(a, b):
    return lax.dot_general(a, b, (((1,), (1,)), ((), ())), preferred_element_type=F32)


def _dot(a, b):
    return jnp.dot(a, b, preferred_element_type=F32)


def _sigmoid(x):
    return 1.0 / (1.0 + jnp.exp(-x))


def _rms(x, w):
    return x * lax.rsqrt(jnp.mean(x * x, axis=-1, keepdims=True) + EPS) * w


def _inproj_kernel(x_ref, nw_ref, w_ref, z_ref, h_scr):
    @pl.when(pl.program_id(1) == 0)
    def _():
        h_scr[...] = _rms(x_ref[...], nw_ref[...]).astype(BF16)

    z_ref[...] = _dot(h_scr[...], w_ref[...]).astype(BF16)


def _inproj(x2, norm_w, w_in_p, tm, tn):
    t = x2.shape[0]
    return pl.pallas_call(
        _inproj_kernel,
        grid=(t // tm, Z_WIDTH // tn),
        in_specs=[
            pl.BlockSpec((tm, D_MODEL), lambda i, j: (i, 0)),
            pl.BlockSpec((1, D_MODEL), lambda i, j: (0, 0)),
            pl.BlockSpec((D_MODEL, tn), lambda i, j: (0, j)),
        ],
        out_specs=pl.BlockSpec((tm, tn), lambda i, j: (i, j)),
        out_shape=jax.ShapeDtypeStruct((t, Z_WIDTH), BF16),
        scratch_shapes=[pltpu.VMEM((tm, D_MODEL), BF16)],
        compiler_params=_cparams(("arbitrary", "arbitrary")),
        name="inproj",
    )(x2, norm_w, w_in_p)


def _ret_kernel(q_ref, k_ref, v_ref, g_ref, cos_ref, sin_ref, dm_ref, zeta_ref, xi_ref,
                dec_ref, nw_ref, y_ref, st_scr, *, n_chunks):
    @pl.when(pl.program_id(1) == 0)
    def _():
        st_scr[...] = jnp.zeros_like(st_scr)

    c = RET_CHUNK
    for ci in range(n_chunks):
        rows = slice(ci * c, (ci + 1) * c)
        cos = cos_ref[rows, :]
        sin = sin_ref[rows, :]
        for h in range(RET_HEADS):
            q = q_ref[rows, h * RET_DK:(h + 1) * RET_DK].astype(F32)
            k = k_ref[rows, h * RET_DK:(h + 1) * RET_DK].astype(F32)
            qr = q * cos + pltpu.roll(q, RET_DK // 2, axis=1) * sin
            kr = (k * cos + pltpu.roll(k, RET_DK // 2, axis=1) * sin) * (RET_DK ** -0.5)
            qb = qr.astype(BF16)
            kb = kr.astype(BF16)
            v = v_ref[rows, h * RET_DV:(h + 1) * RET_DV]
            scores = _nt_dot(qb, kb) * dm_ref[h]
            o = _dot(scores.astype(BF16), v)
            st = st_scr[h]
            o = o + _dot(qb, st.astype(BF16)) * xi_ref[h]
            vz = (v.astype(F32) * zeta_ref[h]).astype(BF16)
            kv = _dot(kr.T.astype(BF16), vz)
            st_scr[h] = st * dec_ref[h] + kv
            mu = jnp.mean(o, axis=-1, keepdims=True)
            d = o - mu
            var = jnp.mean(d * d, axis=-1, keepdims=True)
            on = d * lax.rsqrt(var + EPS) * nw_ref[:, h * RET_DV:(h + 1) * RET_DV]
            g = g_ref[rows, h * RET_DV:(h + 1) * RET_DV].astype(F32)
            y_ref[rows, h * RET_DV:(h + 1) * RET_DV] = (g * _sigmoid(g) * on).astype(BF16)


def _retention_consts(s):
    c = RET_CHUNK
    pos = jnp.arange(s, dtype=F32)
    theta = 10000.0 ** (-jnp.linspace(0.0, 1.0, RET_DK // 2, dtype=F32))
    ang = pos[:, None] * theta[None, :]
    cos = jnp.cos(ang)
    sin = jnp.sin(ang)
    cos2 = jnp.concatenate([cos, cos], axis=-1)
    sin2 = jnp.concatenate([-sin, sin], axis=-1)
    log_gamma = jnp.log1p(-(2.0 ** (-5.0 - jnp.arange(RET_HEADS, dtype=F32))))
    j = jnp.arange(c, dtype=F32)
    rel = j[:, None] - j[None, :]
    dmask = jnp.where(rel >= 0, jnp.exp(log_gamma[:, None, None] * jnp.maximum(rel, 0.0)), 0.0)
    zeta = jnp.exp(log_gamma[:, None] * (c - 1 - j)[None, :])
    xi = jnp.exp(log_gamma[:, None] * (j + 1.0)[None, :])
    dec = jnp.exp(log_gamma * c)
    zeta_b = jnp.broadcast_to(zeta[:, :, None], (RET_HEADS, c, RET_DV))
    xi_b = jnp.broadcast_to(xi[:, :, None], (RET_HEADS, c, RET_DV))
    dec_b = jnp.broadcast_to(dec[:, None, None], (RET_HEADS, RET_DK, RET_DV))
    return cos2, sin2, dmask, zeta_b, xi_b, dec_b


def _retention(z, ret_norm_w, consts, b, s, tr):
    t = b * s
    nt = s // tr
    cos2, sin2, dmask, zeta_b, xi_b, dec_b = consts
    const3 = lambda bi, i: (0, 0, 0)
    return pl.pallas_call(
        functools.partial(_ret_kernel, n_chunks=tr // RET_CHUNK),
        grid=(b, nt),
        in_specs=[
            pl.BlockSpec((tr, 512), lambda bi, i: (bi * nt + i, OFF_RQ // 512)),
            pl.BlockSpec((tr, 512), lambda bi, i: (bi * nt + i, OFF_RK // 512)),
            pl.BlockSpec((tr, 1024), lambda bi, i: (bi * nt + i, OFF_RV // 1024)),
            pl.BlockSpec((tr, 1024), lambda bi, i: (bi * nt + i, OFF_RG // 1024)),
            pl.BlockSpec((tr, RET_DK), lambda bi, i: (i, 0)),
            pl.BlockSpec((tr, RET_DK), lambda bi, i: (i, 0)),
            pl.BlockSpec((RET_HEADS, RET_CHUNK, RET_CHUNK), const3),
            pl.BlockSpec((RET_HEADS, RET_CHUNK, RET_DV), const3),
            pl.BlockSpec((RET_HEADS, RET_CHUNK, RET_DV), const3),
            pl.BlockSpec((RET_HEADS, RET_DK, RET_DV), const3),
            pl.BlockSpec((1, D_MODEL), lambda bi, i: (0, 0)),
        ],
        out_specs=pl.BlockSpec((tr, D_MODEL), lambda bi, i: (bi * nt + i, 0)),
        out_shape=jax.ShapeDtypeStruct((t, D_MODEL), BF16),
        scratch_shapes=[pltpu.VMEM((RET_HEADS, RET_DK, RET_DV), F32)],
        compiler_params=_cparams(("arbitrary", "arbitrary")),
        name="retention",
    )(z, z, z, z, cos2, sin2, dmask, zeta_b, xi_b, dec_b, ret_norm_w)


def _compress_kernel(blk_ref, pos_ref, w1_ref, w2_ref, out_ref):
    blk = (blk_ref[0, 0].astype(F32) + pos_ref[0]).astype(BF16)
    hid = _dot(blk, w1_ref[0])
    gel = 0.5 * hid * (1.0 + jnp.tanh(np.sqrt(2.0 / np.pi) * (hid + 0.044715 * (hid * hid * hid))))
    out_ref[0, 0] = _dot(gel.astype(BF16), w2_ref[0]).astype(BF16)


def _compress(blk, pos_flat, w1, w2p):
    _, bg, ncp, width = blk.shape
    return pl.pallas_call(
        _compress_kernel,
        grid=(2, bg),
        in_specs=[
            pl.BlockSpec((1, 1, ncp, width), lambda a, n: (a, n, 0, 0)),
            pl.BlockSpec((1, 1, width), lambda a, n: (a, 0, 0)),
            pl.BlockSpec((1, width, CMP_HIDDEN), lambda a, n: (a, 0, 0)),
            pl.BlockSpec((1, CMP_HIDDEN, LANES), lambda a, n: (a, 0, 0)),
        ],
        out_specs=pl.BlockSpec((1, 1, ncp, LANES), lambda a, n: (a, n, 0, 0)),
        out_shape=jax.ShapeDtypeStruct((2, bg, ncp, LANES), BF16),
        compiler_params=_cparams(("arbitrary", "arbitrary")),
        name="nsa_compress",
    )(blk, pos_flat, w1, w2p)


def _head_pad(q2_ref_block, h):
    x = q2_ref_block[:, (h // 2) * LANES:(h // 2 + 1) * LANES].astype(F32)
    if h % 2:
        x = pltpu.roll(x, NSA_DH, axis=1)
    lane = lax.broadcasted_iota(jnp.int32, x.shape, 1)
    return jnp.where(lane < NSA_DH, x, 0.0)


def _merge_heads(parts):
    lo = parts[0] + pltpu.roll(parts[1], NSA_DH, axis=1)
    hi = parts[2] + pltpu.roll(parts[3], NSA_DH, axis=1)
    return jnp.concatenate([lo, hi], axis=1)


def _cmp_select_kernel(q_ref, kc_ref, vc_ref, ovt_ref, ocmp_ref, qaug_ref, *, tq, n_blk):
    t0 = pl.program_id(1) * tq
    ncp = kc_ref.shape[2]
    kc = kc_ref[0, 0]
    vc = vc_ref[0, 0]
    ovt = ovt_ref[...]
    scale = NSA_DH ** -0.5

    t_q = t0 + lax.broadcasted_iota(jnp.int32, (tq, ncp), 0)
    c_id = lax.broadcasted_iota(jnp.int32, (tq, ncp), 1)
    valid = (c_id * CMP_STRIDE + (CMP_BLOCK - 1)) <= t_q
    t_col = t0 + lax.broadcasted_iota(jnp.int32, (tq, 1), 0)
    any1 = jnp.where(t_col >= CMP_BLOCK - 1, 1.0, 0.0)

    imp = jnp.zeros((n_blk, tq), F32)
    q_pads = []
    o_parts = []
    for h in range(NSA_HPG):
        qp = _head_pad(q_ref, h) * scale
        q_pads.append(qp)
        s1 = jnp.where(valid, _nt_dot(qp.astype(BF16), kc), NEG)
        m = jnp.max(s1, axis=-1, keepdims=True)
        e = jnp.exp(s1 - m)
        p1 = e * (1.0 / jnp.sum(e, axis=-1, keepdims=True)) * any1
        pb = p1.astype(BF16)
        o_parts.append(_dot(pb, vc))
        imp = imp + _nt_dot(ovt, pb)
    ocmp_ref[...] = _merge_heads(o_parts).astype(BF16)

    blk = lax.broadcasted_iota(jnp.int32, (n_blk, tq), 0)
    t_row = t0 + lax.broadcasted_iota(jnp.int32, (n_blk, tq), 1)
    cur = t_row // SLC_BLOCK
    causal = blk <= cur
    forced = (blk == 0) | (blk == cur) | (blk == cur - 1)
    imp = jnp.where(forced, FORCE, imp)
    imp = jnp.where(causal, imp, NEG)
    rank = jnp.zeros((n_blk, tq), F32)
    for i in range(n_blk):
        r = imp[i:i + 1, :]
        rank = rank + jnp.where(blk > i, jnp.where(r >= imp, 1.0, 0.0), jnp.where(r > imp, 1.0, 0.0))
    bias_t = jnp.where(rank < float(min(N_SELECT, n_blk)), 0.0, NEG)
    pad_lo = jnp.zeros((NSA_DH, tq), F32)
    pieces = [pad_lo, bias_t]
    if n_blk < LANES - NSA_DH:
        pieces.append(jnp.full((LANES - NSA_DH - n_blk, tq), NEG, F32))
    bias_q = jnp.concatenate(pieces, axis=0).T
    for h in range(NSA_HPG):
        qaug_ref[0, h] = (q_pads[h] + bias_q).astype(BF16)


def _cmp_select(z, kvc, ovt, b, s, tq):
    t = b * s
    nq = s // tq
    bg = b * NSA_KV_HEADS
    ncp = kvc.shape[2]
    n_blk = s // SLC_BLOCK
    qcol = OFF_NQ // 256
    return pl.pallas_call(
        functools.partial(_cmp_select_kernel, tq=tq, n_blk=n_blk),
        grid=(bg, nq),
        in_specs=[
            pl.BlockSpec((tq, 256), lambda n, i: ((n // NSA_KV_HEADS) * nq + i, qcol + n % NSA_KV_HEADS)),
            pl.BlockSpec((1, 1, ncp, LANES), lambda n, i: (0, n, 0, 0)),
            pl.BlockSpec((1, 1, ncp, LANES), lambda n, i: (1, n, 0, 0)),
            pl.BlockSpec((n_blk, ncp), lambda n, i: (0, 0)),
        ],
        out_specs=[
            pl.BlockSpec((tq, 256), lambda n, i: ((n // NSA_KV_HEADS) * nq + i, n % NSA_KV_HEADS)),
            pl.BlockSpec((1, NSA_HPG, tq, LANES), lambda n, i: (n, 0, i, 0)),
        ],
        out_shape=[
            jax.ShapeDtypeStruct((t, D_MODEL), BF16),
            jax.ShapeDtypeStruct((bg, NSA_HPG, s, LANES), BF16),
        ],
        compiler_params=_cparams(("arbitrary", "arbitrary")),
        name="nsa_cmp_select",
    )(z, kvc, kvc, ovt)


def _flash_kernel(q_ref, k_ref, v_ref, o_ref, m_scr, acc_scr, *, tq, tk, mode):
    t0 = pl.program_id(1) * tq
    rows = NSA_HPG * tq
    q = q_ref[0].reshape(rows, LANES)
    m_scr[...] = jnp.full(m_scr.shape, NEG, F32)
    acc_scr[...] = jnp.zeros(acc_scr.shape, F32)

    def step(kt, masked):
        start = pl.multiple_of(kt * tk, tk)
        k = k_ref[0, pl.ds(start, tk), :]
        v = v_ref[0, pl.ds(start, tk), :]
        sc = _nt_dot(q, k)
        if masked:
            r = lax.broadcasted_iota(jnp.int32, (rows, tk), 0)
            t_q = t0 + (r & (tq - 1))
            kpos = start + lax.broadcasted_iota(jnp.int32, (rows, tk), 1)
            ok = kpos <= t_q
            if mode == "win":
                ok = ok & (kpos > t_q - WINDOW)
            sc = jnp.where(ok, sc, NEG)
        m_prev = m_scr[...]
        m_new = jnp.maximum(m_prev, jnp.max(sc, axis=-1, keepdims=True))
        alpha = jnp.exp(m_prev - m_new)
        p = jnp.exp(sc - jnp.concatenate([m_new] * (tk // LANES), axis=1))
        acc_scr[...] = alpha * acc_scr[...] + _dot(p.astype(BF16), v)
        m_scr[...] = m_new

    if mode == "sel":
        n_full = t0 // tk

        def body(kt, carry):
            step(kt, False)
            return carry

        lax.fori_loop(0, n_full, body, 0)
        step(n_full, True)
    else:
        lo = jnp.maximum(t0 - (WINDOW - 1), 0) // tk
        hi = (t0 + tq - 1) // tk

        def body(kt, carry):
            step(kt, True)
            return carry

        lax.fori_loop(lo, hi + 1, body, 0)

    acc = acc_scr[...]
    lane = lax.broadcasted_iota(jnp.int32, acc.shape, 1)
    denom = jnp.sum(jnp.where(lane == NSA_DH, acc, 0.0), axis=-1, keepdims=True)
    out = jnp.where(lane < NSA_DH, acc * (1.0 / denom), 0.0)
    o_ref[...] = _merge_heads([out[h * tq:(h + 1) * tq] for h in range(NSA_HPG)]).astype(BF16)


def _flash(qaug, k_aug, v_aug, b, s, tq, tk, mode):
    t = b * s
    nq = s // tq
    bg = b * NSA_KV_HEADS
    return pl.pallas_call(
        functools.partial(_flash_kernel, tq=tq, tk=tk, mode=mode),
        grid=(bg, nq),
        in_specs=[
            pl.BlockSpec((1, NSA_HPG, tq, LANES), lambda n, i: (n, 0, i, 0)),
            pl.BlockSpec((1, s, LANES), lambda n, i: (n, 0, 0)),
            pl.BlockSpec((1, s, LANES), lambda n, i: (n, 0, 0)),
        ],
        out_specs=pl.BlockSpec((tq, 256), lambda n, i: ((n // NSA_KV_HEADS) * nq + i, n % NSA_KV_HEADS)),
        out_shape=jax.ShapeDtypeStruct((t, D_MODEL), BF16),
        scratch_shapes=[pltpu.VMEM((NSA_HPG * tq, LANES), F32), pltpu.VMEM((NSA_HPG * tq, LANES), F32)],
        compiler_params=_cparams(("arbitrary", "arbitrary")),
        name="nsa_flash_" + mode,
    )(qaug, k_aug, v_aug)


def _shifted(u, prev, n):
    rolled = pltpu.roll(u, n, axis=0)
    row = lax.broadcasted_iota(jnp.int32, u.shape, 0)
    out = rolled
    for r in range(n):
        out = jnp.where(row == r, prev[HALO_ROWS - n + r:HALO_ROWS - n + r + 1, :], out)
    return out


def _causal_conv3(u, prev, w):
    return w[0:1, :] * _shifted(u, prev, 2) + w[1:2, :] * _shifted(u, prev, 1) + w[2:3, :] * u


def _merge_kernel(x_ref, yret_ref, sb_ref, sc_ref, sx_ref, sch_ref, sxh_ref, mg0_ref, mg1_ref, mg2_ref,
                  ng_ref, ocmp_ref, oslc_ref, owin_ref, cw_ref, e_ref, wret_ref, wsc_ref, wnsa_ref,
                  wmix_ref, out_ref, *, tiles_per_seq):
    first = (pl.program_id(0) % tiles_per_seq) == 0
    u = sc_ref[...].astype(F32) * sx_ref[...].astype(F32)
    prev = sch_ref[...].astype(F32) * sxh_ref[...].astype(F32)
    prev = jnp.where(first, 0.0, prev)
    y_sc = (sb_ref[...].astype(F32) * _causal_conv3(u, prev, cw_ref[...])).astype(BF16)

    gates = _dot(_sigmoid(ng_ref[...].astype(F32)).astype(BF16), e_ref[...])
    y_nsa = (gates[:, 0:D_MODEL] * ocmp_ref[...].astype(F32)
             + gates[:, D_MODEL:2 * D_MODEL] * oslc_ref[...].astype(F32)
             + gates[:, 2 * D_MODEL:3 * D_MODEL] * owin_ref[...].astype(F32)).astype(BF16)

    merged = (_sigmoid(mg0_ref[...].astype(F32)) * _dot(yret_ref[...], wret_ref[...])
              + _sigmoid(mg1_ref[...].astype(F32)) * _dot(y_sc, wsc_ref[...])
              + _sigmoid(mg2_ref[...].astype(F32)) * _dot(y_nsa, wnsa_ref[...]))
    out_ref[...] = x_ref[...] + _dot(merged.astype(BF16), wmix_ref[...])


def _merge(x2, z, y_ret, o_cmp, o_slc, o_win, conv_w, gate_expand, w_ret, w_sc, w_nsa, w_mix, s, tm):
    t = x2.shape[0]
    row = lambda c: (lambda i: (i, c))
    halo = lambda c: (lambda i: (jnp.maximum(i * (tm // HALO_ROWS) - 1, 0), c))
    const = lambda i: (0, 0)
    wspec = pl.BlockSpec((D_MODEL, D_MODEL), const)
    return pl.pallas_call(
        functools.partial(_merge_kernel, tiles_per_seq=s // tm),
        grid=(t // tm,),
        in_specs=[
            pl.BlockSpec((tm, D_MODEL), row(0)),
            pl.BlockSpec((tm, D_MODEL), row(0)),
            pl.BlockSpec((tm, D_MODEL), row(OFF_SB // D_MODEL)),
            pl.BlockSpec((tm, D_MODEL), row(OFF_SC // D_MODEL)),
            pl.BlockSpec((tm, D_MODEL), row(OFF_SX // D_MODEL)),
            pl.BlockSpec((HALO_ROWS, D_MODEL), halo(OFF_SC // D_MODEL)),
            pl.BlockSpec((HALO_ROWS, D_MODEL), halo(OFF_SX // D_MODEL)),
            pl.BlockSpec((tm, D_MODEL), row(OFF_MG // D_MODEL)),
            pl.BlockSpec((tm, D_MODEL), row(OFF_MG // D_MODEL + 1)),
            pl.BlockSpec((tm, D_MODEL), row(OFF_MG // D_MODEL + 2)),
            pl.BlockSpec((tm, LANES), row(OFF_NG // LANES)),
            pl.BlockSpec((tm, D_MODEL), row(0)),
            pl.BlockSpec((tm, D_MODEL), row(0)),
            pl.BlockSpec((tm, D_MODEL), row(0)),
            pl.BlockSpec((CONV_WIDTH, D_MODEL), const),
            pl.BlockSpec((LANES, 3 * D_MODEL), const),
            wspec, wspec, wspec, wspec,
        ],
        out_specs=pl.BlockSpec((tm, D_MODEL), row(0)),
        out_shape=jax.ShapeDtypeStruct((t, D_MODEL), F32),
        compiler_params=_cparams(("arbitrary",)),
        name="merge",
    )(x2, y_ret, z, z, z, z, z, z, z, z, z, o_cmp, o_slc, o_win, conv_w, gate_expand,
      w_ret, w_sc, w_nsa, w_mix)


def _ffn_kernel(x_ref, nw_ref, wa_ref, wv_ref, cwa_ref, cwv_ref, wd_ref, fw_ref, out_ref,
                h_scr, acc_scr, ca_scr, cv_scr, *, tiles_per_seq, final_norm):
    i = pl.program_id(0)
    j = pl.program_id(1)
    first = (i % tiles_per_seq) == 0

    @pl.when(j == 0)
    def _():
        h_scr[...] = _rms(x_ref[...], nw_ref[...]).astype(BF16)
        acc_scr[...] = jnp.zeros_like(acc_scr)

    h = h_scr[...]
    tm = h.shape[0]
    ua = _dot(h, wa_ref[...])
    uv = _dot(h, wv_ref[...])
    @pl.when(first)
    def _():
        ca_scr[j] = jnp.zeros(ca_scr.shape[1:], F32)
        cv_scr[j] = jnp.zeros(cv_scr.shape[1:], F32)

    prev_a = ca_scr[j]
    prev_v = cv_scr[j]
    ca_scr[j] = ua[tm - HALO_ROWS:, :]
    cv_scr[j] = uv[tm - HALO_ROWS:, :]
    a = _causal_conv3(ua, prev_a, cwa_ref[...])
    v = _causal_conv3(uv, prev_v, cwv_ref[...])
    act = (a * _sigmoid(a) * v).astype(BF16)
    acc_scr[...] += _dot(act, wd_ref[...])

    @pl.when(j == pl.num_programs(1) - 1)
    def _():
        y = x_ref[...] + acc_scr[...]
        if final_norm:
            y = _rms(y, fw_ref[...])
        out_ref[...] = y


def _ffn(x2, norm_w, w_up, conv_w, w_down, final_w, s, tm, tf, final_norm):
    t = x2.shape[0]
    nf = D_FF // tf
    return pl.pallas_call(
        functools.partial(_ffn_kernel, tiles_per_seq=s // tm, final_norm=final_norm),
        grid=(t // tm, nf),
        in_specs=[
            pl.BlockSpec((tm, D_MODEL), lambda i, j: (i, 0)),
            pl.BlockSpec((1, D_MODEL), lambda i, j: (0, 0)),
            pl.BlockSpec((D_MODEL, tf), lambda i, j: (0, j)),
            pl.BlockSpec((D_MODEL, tf), lambda i, j: (0, nf + j)),
            pl.BlockSpec((CONV_WIDTH, tf), lambda i, j: (0, j)),
            pl.BlockSpec((CONV_WIDTH, tf), lambda i, j: (0, nf + j)),
            pl.BlockSpec((tf, D_MODEL), lambda i, j: (j, 0)),
            pl.BlockSpec((1, D_MODEL), lambda i, j: (0, 0)),
        ],
        out_specs=pl.BlockSpec((tm, D_MODEL), lambda i, j: (i, 0)),
        out_shape=jax.ShapeDtypeStruct((t, D_MODEL), F32),
        scratch_shapes=[
            pltpu.VMEM((tm, D_MODEL), BF16),
            pltpu.VMEM((tm, D_MODEL), F32),
            pltpu.VMEM((nf, HALO_ROWS, tf), F32),
            pltpu.VMEM((nf, HALO_ROWS, tf), F32),
        ],
        compiler_params=_cparams(("arbitrary", "arbitrary")),
        name="ffn",
    )(x2, norm_w, w_up, w_up, conv_w, conv_w, w_down, final_w)


def _prep_w_in(w_in_l):
    n_kv = 6 * NSA_KV_HEADS * NSA_DH
    nkv = w_in_l[:, OFF_MG:OFF_MG + n_kv]
    ng = w_in_l[:, OFF_MG + n_kv:OFF_MG + n_kv + 3 * NSA_HEADS]
    ng = ng.reshape(D_MODEL, NSA_HEADS, 3).transpose(0, 2, 1).reshape(D_MODEL, 3 * NSA_HEADS)
    mg = w_in_l[:, OFF_MG + n_kv + 3 * NSA_HEADS:]
    pad = jnp.zeros((D_MODEL, Z_WIDTH - OFF_NG - 3 * NSA_HEADS), w_in_l.dtype)
    return jnp.concatenate([w_in_l[:, :OFF_MG], mg, nkv, ng, pad], axis=1).astype(BF16)


def _gate_expander():
    e = np.zeros((LANES, 3 * D_MODEL), np.float32)
    for br in range(3):
        for h in range(NSA_HEADS):
            e[br * NSA_HEADS + h, br * D_MODEL + h * NSA_DH: br * D_MODEL + (h + 1) * NSA_DH] = 1.0
    return jnp.asarray(e, BF16)


def _overlap_t(s, ncp):
    nc = (s - CMP_BLOCK) // CMP_STRIDE + 1
    ns = s // SLC_BLOCK
    ci = np.arange(ncp) * CMP_STRIDE
    sj = np.arange(ns) * SLC_BLOCK
    ov = ((ci[None, :] < sj[:, None] + SLC_BLOCK) & (ci[None, :] + CMP_BLOCK > sj[:, None])
          & (np.arange(ncp)[None, :] < nc))
    return jnp.asarray(ov.astype(np.float32), BF16)


def _group_major(a, b, s):
    return a.reshape(b, s, NSA_KV_HEADS, NSA_DH).transpose(0, 2, 1, 3).reshape(b * NSA_KV_HEADS, s, NSA_DH)


def kernel(x, attn_norm_w, w_in, ret_norm_w, w_ret_out, sc_conv_w, w_sc_out, nsa_cmp_pos, nsa_cmp_w1,
           nsa_cmp_w2, w_nsa_out, w_mix_out, ffn_norm_w, w_ffn_up, ffn_conv_w, w_ffn_down, final_norm_w):
    b, s, _ = x.shape
    t = b * s
    depth = w_in.shape[0]
    bg = b * NSA_KV_HEADS
    ncp = s // CMP_STRIDE
    assert s % 256 == 0 and s // SLC_BLOCK <= NSA_DH

    tm_in = min(1024, t)
    tm_merge = 256
    tm_ffn = min(1024, s)
    tq_cmp = 128
    tq_att = 128
    tk_att = 256
    tr = 256

    ret_consts = _retention_consts(s)
    gate_expand = _gate_expander()
    ovt = _overlap_t(s, ncp)
    blk_onehot = jnp.asarray(np.eye(NSA_DH, dtype=np.float32)[np.arange(s) // SLC_BLOCK], BF16)
    ones_col = jnp.zeros((s, NSA_DH), BF16).at[:, 0].set(1.0)
    zeros_col = jnp.zeros((s, NSA_DH), BF16)
    bcast = lambda c: jnp.broadcast_to(c[None], (bg, s, NSA_DH))

    x2 = x.reshape(t, D_MODEL)
    for l in range(depth):
        z = _inproj(x2, attn_norm_w[l][None], _prep_w_in(w_in[l]), tm_in, 1024)
        y_ret = _retention(z, ret_norm_w[l][None], ret_consts, b, s, tr)

        kv = [_group_major(z[:, OFF_NKV + 256 * n: OFF_NKV + 256 * (n + 1)], b, s) for n in range(6)]
        k_c, v_c, k_s, v_s, k_w, v_w = kv

        def blocks(a):
            r = a.reshape(bg, ncp, CMP_STRIDE * NSA_DH)
            nxt = jnp.concatenate([r[:, 1:], jnp.zeros_like(r[:, :1])], axis=1)
            return jnp.concatenate([r, nxt], axis=-1)

        blk = jnp.stack([blocks(k_c), blocks(v_c)])
        pos_flat = nsa_cmp_pos[l].reshape(2, 1, CMP_BLOCK * NSA_DH)
        w2p = jnp.pad(nsa_cmp_w2[l], ((0, 0), (0, 0), (0, LANES - NSA_DH))).astype(BF16)
        kvc = _compress(blk, pos_flat, nsa_cmp_w1[l].astype(BF16), w2p)

        o_cmp, qaug = _cmp_select(z, kvc, ovt, b, s, tq_cmp)
        ks_aug = jnp.concatenate([k_s, bcast(blk_onehot)], axis=-1)
        vs_aug = jnp.concatenate([v_s, bcast(ones_col)], axis=-1)
        kw_aug = jnp.concatenate([k_w, bcast(zeros_col)], axis=-1)
        vw_aug = jnp.concatenate([v_w, bcast(ones_col)], axis=-1)
        o_slc = _flash(qaug, ks_aug, vs_aug, b, s, tq_att, tk_att, "sel")
        o_win = _flash(qaug, kw_aug, vw_aug, b, s, tq_att, tk_att, "win")

        x2 = _merge(x2, z, y_ret, o_cmp, o_slc, o_win, sc_conv_w[l], gate_expand,
                    w_ret_out[l].astype(BF16), w_sc_out[l].astype(BF16), w_nsa_out[l].astype(BF16),
                    w_mix_out[l].astype(BF16), s, tm_merge)
        x2 = _ffn(x2, ffn_norm_w[l][None], w_ffn_up[l].astype(BF16), ffn_conv_w[l],
                  w_ffn_down[l].astype(BF16), final_norm_w[None], s, tm_ffn, 256, l == depth - 1)
    return x2.reshape(b, s, D_MODEL)
```

```python
import functools

import numpy as np
import jax
import jax.numpy as jnp
from jax import lax
from jax.experimental import pallas as pl
from jax.experimental.pallas import tpu as pltpu

F32 = jnp.float32
BF16 = jnp.bfloat16

D_MODEL = 1024
RET_HEADS = 4
RET_DV = D_MODEL // RET_HEADS
RET_DK = RET_DV // 2
RET_CHUNK = 128
CONV_WIDTH = 3
NSA_DH = 64
NSA_HEADS = D_MODEL // NSA_DH
NSA_KV_HEADS = 4
NSA_HPG = NSA_HEADS // NSA_KV_HEADS
CMP_BLOCK = 32
CMP_STRIDE = 16
CMP_HIDDEN = 256
SLC_BLOCK = 64
N_SELECT = 16
WINDOW = 512
D_FF = ((8 * D_MODEL // 3 + 127) // 128) * 128
EPS = 1e-6
NEG = -1e30
FORCE = 1e6

LANES = 128
HALO_ROWS = 8

OFF_RQ = 0
OFF_RK = 512
OFF_RV = 1024
OFF_RG = 2048
OFF_SB = 3072
OFF_SC = 4096
OFF_SX = 5120
OFF_NQ = 6144
OFF_MG = 7168
OFF_NKV = 10240
OFF_NG = 11776
Z_WIDTH = 12288

VMEM_LIMIT = 56 * 1024 * 1024


def _cparams(sem):
    return pltpu.CompilerParams(dimension_semantics=sem, vmem_limit_bytes=VMEM_LIMIT)


def _nt_dot(a, b):
    return lax.dot_general(a, b, (((1,), (1,)), ((), ())), preferred_element_type=F32)


def _dot(a, b):
    return jnp.dot(a, b, preferred_element_type=F32)


def _sigmoid(x):
    return 1.0 / (1.0 + jnp.exp(-x))


def _rms(x, w):
    return x * lax.rsqrt(jnp.mean(x * x, axis=-1, keepdims=True) + EPS) * w


def _inproj_kernel(x_ref, nw_ref, w_ref, z_ref, h_scr):
    @pl.when(pl.program_id(1) == 0)
    def _():
        h_scr[...] = _rms(x_ref[...], nw_ref[...]).astype(BF16)

    z_ref[...] = _dot(h_scr[...], w_ref[...]).astype(BF16)


def _inproj(x2, norm_w, w_in_p, tm, tn):
    t = x2.shape[0]
    return pl.pallas_call(
        _inproj_kernel,
        grid=(t // tm, Z_WIDTH // tn),
        in_specs=[
            pl.BlockSpec((tm, D_MODEL), lambda i, j: (i, 0)),
            pl.BlockSpec((1, D_MODEL), lambda i, j: (0, 0)),
            pl.BlockSpec((D_MODEL, tn), lambda i, j: (0, j)),
        ],
        out_specs=pl.BlockSpec((tm, tn), lambda i, j: (i, j)),
        out_shape=jax.ShapeDtypeStruct((t, Z_WIDTH), BF16),
        scratch_shapes=[pltpu.VMEM((tm, D_MODEL), BF16)],
        compiler_params=_cparams(("arbitrary", "arbitrary")),
        name="inproj",
    )(x2, norm_w, w_in_p)


def _ret_kernel(q_ref, k_ref, v_ref, g_ref, cos_ref, sin_ref, dm_ref, zeta_ref, xi_ref,
                dec_ref, nw_ref, y_ref, st_scr, *, n_chunks):
    @pl.when(pl.program_id(1) == 0)
    def _():
        st_scr[...] = jnp.zeros_like(st_scr)

    c = RET_CHUNK
    for ci in range(n_chunks):
        rows = slice(ci * c, (ci + 1) * c)
        cos = cos_ref[rows, :]
        sin = sin_ref[rows, :]
        for h in range(RET_HEADS):
            q = q_ref[rows, h * RET_DK:(h + 1) * RET_DK].astype(F32)
            k = k_ref[rows, h * RET_DK:(h + 1) * RET_DK].astype(F32)
            qr = q * cos + pltpu.roll(q, RET_DK // 2, axis=1) * sin
            kr = (k * cos + pltpu.roll(k, RET_DK // 2, axis=1) * sin) * (RET_DK ** -0.5)
            qb = qr.astype(BF16)
            kb = kr.astype(BF16)
            v = v_ref[rows, h * RET_DV:(h + 1) * RET_DV]
            scores = _nt_dot(qb, kb) * dm_ref[h]
            o = _dot(scores.astype(BF16), v)
            st = st_scr[h]
            o = o + _dot(qb, st.astype(BF16)) * xi_ref[h]
            vz = (v.astype(F32) * zeta_ref[h]).astype(BF16)
            kv = _dot(kr.T.astype(BF16), vz)
            st_scr[h] = st * dec_ref[h] + kv
            mu = jnp.mean(o, axis=-1, keepdims=True)
            d = o - mu
            var = jnp.mean(d * d, axis=-1, keepdims=True)
            on = d * lax.rsqrt(var + EPS) * nw_ref[:, h * RET_DV:(h + 1) * RET_DV]
            g = g_ref[rows, h * RET_DV:(h + 1) * RET_DV].astype(F32)
            y_ref[rows, h * RET_DV:(h + 1) * RET_DV] = (g * _sigmoid(g) * on).astype(BF16)


def _retention_consts(s):
    c = RET_CHUNK
    pos = jnp.arange(s, dtype=F32)
    theta = 10000.0 ** (-jnp.linspace(0.0, 1.0, RET_DK // 2, dtype=F32))
    ang = pos[:, None] * theta[None, :]
    cos = jnp.cos(ang)
    sin = jnp.sin(ang)
    cos2 = jnp.concatenate([cos, cos], axis=-1)
    sin2 = jnp.concatenate([-sin, sin], axis=-1)
    log_gamma = jnp.log1p(-(2.0 ** (-5.0 - jnp.arange(RET_HEADS, dtype=F32))))
    j = jnp.arange(c, dtype=F32)
    rel = j[:, None] - j[None, :]
    dmask = jnp.where(rel >= 0, jnp.exp(log_gamma[:, None, None] * jnp.maximum(rel, 0.0)), 0.0)
    zeta = jnp.exp(log_gamma[:, None] * (c - 1 - j)[None, :])
    xi = jnp.exp(log_gamma[:, None] * (j + 1.0)[None, :])
    dec = jnp.exp(log_gamma * c)
    zeta_b = jnp.broadcast_to(zeta[:, :, None], (RET_HEADS, c, RET_DV))
    xi_b = jnp.broadcast_to(xi[:, :, None], (RET_HEADS, c, RET_DV))
    dec_b = jnp.broadcast_to(dec[:, None, None], (RET_HEADS, RET_DK, RET_DV))
    return cos2, sin2, dmask, zeta_b, xi_b, dec_b


def _retention(z, ret_norm_w, consts, b, s, tr):
    t = b * s
    nt = s // tr
    cos2, sin2, dmask, zeta_b, xi_b, dec_b = consts
    const3 = lambda bi, i: (0, 0, 0)
    return pl.pallas_call(
        functools.partial(_ret_kernel, n_chunks=tr // RET_CHUNK),
        grid=(b, nt),
        in_specs=[
            pl.BlockSpec((tr, 512), lambda bi, i: (bi * nt + i, OFF_RQ // 512)),
            pl.BlockSpec((tr, 512), lambda bi, i: (bi * nt + i, OFF_RK // 512)),
            pl.BlockSpec((tr, 1024), lambda bi, i: (bi * nt + i, OFF_RV // 1024)),
            pl.BlockSpec((tr, 1024), lambda bi, i: (bi * nt + i, OFF_RG // 1024)),
            pl.BlockSpec((tr, RET_DK), lambda bi, i: (i, 0)),
            pl.BlockSpec((tr, RET_DK), lambda bi, i: (i, 0)),
            pl.BlockSpec((RET_HEADS, RET_CHUNK, RET_CHUNK), const3),
            pl.BlockSpec((RET_HEADS, RET_CHUNK, RET_DV), const3),
            pl.BlockSpec((RET_HEADS, RET_CHUNK, RET_DV), const3),
            pl.BlockSpec((RET_HEADS, RET_DK, RET_DV), const3),
            pl.BlockSpec((1, D_MODEL), lambda bi, i: (0, 0)),
        ],
        out_specs=pl.BlockSpec((tr, D_MODEL), lambda bi, i: (bi * nt + i, 0)),
        out_shape=jax.ShapeDtypeStruct((t, D_MODEL), BF16),
        scratch_shapes=[pltpu.VMEM((RET_HEADS, RET_DK, RET_DV), F32)],
        compiler_params=_cparams(("arbitrary", "arbitrary")),
        name="retention",
    )(z, z, z, z, cos2, sin2, dmask, zeta_b, xi_b, dec_b, ret_norm_w)


def _compress_kernel(blk_ref, pos_ref, w1_ref, w2_ref, out_ref):
    blk = (blk_ref[0, 0].astype(F32) + pos_ref[0]).astype(BF16)
    hid = _dot(blk, w1_ref[0])
    gel = 0.5 * hid * (1.0 + jnp.tanh(np.sqrt(2.0 / np.pi) * (hid + 0.044715 * (hid * hid * hid))))
    out_ref[0, 0] = _dot(gel.astype(BF16), w2_ref[0]).astype(BF16)


def _compress(blk, pos_flat, w1, w2p):
    _, bg, ncp, width = blk.shape
    return pl.pallas_call(
        _compress_kernel,
        grid=(2, bg),
        in_specs=[
            pl.BlockSpec((1, 1, ncp, width), lambda a, n: (a, n, 0, 0)),
            pl.BlockSpec((1, 1, width), lambda a, n: (a, 0, 0)),
            pl.BlockSpec((1, width, CMP_HIDDEN), lambda a, n: (a, 0, 0)),
            pl.BlockSpec((1, CMP_HIDDEN, LANES), lambda a, n: (a, 0, 0)),
        ],
        out_specs=pl.BlockSpec((1, 1, ncp, LANES), lambda a, n: (a, n, 0, 0)),
        out_shape=jax.ShapeDtypeStruct((2, bg, ncp, LANES), BF16),
        compiler_params=_cparams(("arbitrary", "arbitrary")),
        name="nsa_compress",
    )(blk, pos_flat, w1, w2p)


def _head_pad(q2_ref_block, h):
    x = q2_ref_block[:, (h // 2) * LANES:(h // 2 + 1) * LANES].astype(F32)
    if h % 2:
        x = pltpu.roll(x, NSA_DH, axis=1)
    lane = lax.broadcasted_iota(jnp.int32, x.shape, 1)
    return jnp.where(lane < NSA_DH, x, 0.0)


def _merge_heads(parts):
    lo = parts[0] + pltpu.roll(parts[1], NSA_DH, axis=1)
    hi = parts[2] + pltpu.roll(parts[3], NSA_DH, axis=1)
    return jnp.concatenate([lo, hi], axis=1)


def _cmp_select_kernel(q_ref, kc_ref, vc_ref, ovt_ref, ocmp_ref, qaug_ref, *, tq, n_blk):
    t0 = pl.program_id(1) * tq
    ncp = kc_ref.shape[2]
    kc = kc_ref[0, 0]
    vc = vc_ref[0, 0]
    ovt = ovt_ref[...]
    scale = NSA_DH ** -0.5

    t_q = t0 + lax.broadcasted_iota(jnp.int32, (tq, ncp), 0)
    c_id = lax.broadcasted_iota(jnp.int32, (tq, ncp), 1)
    valid = (c_id * CMP_STRIDE + (CMP_BLOCK - 1)) <= t_q
    t_col = t0 + lax.broadcasted_iota(jnp.int32, (tq, 1), 0)
    any1 = jnp.where(t_col >= CMP_BLOCK - 1, 1.0, 0.0)

    imp = jnp.zeros((n_blk, tq), F32)
    q_pads = []
    o_parts = []
    for h in range(NSA_HPG):
        qp = _head_pad(q_ref, h) * scale
        q_pads.append(qp)
        s1 = jnp.where(valid, _nt_dot(qp.astype(BF16), kc), NEG)
        m = jnp.max(s1, axis=-1, keepdims=True)
        e = jnp.exp(s1 - m)
        p1 = e * (1.0 / jnp.sum(e, axis=-1, keepdims=True)) * any1
        pb = p1.astype(BF16)
        o_parts.append(_dot(pb, vc))
        imp = imp + _nt_dot(ovt, pb)
    ocmp_ref[...] = _merge_heads(o_parts).astype(BF16)

    blk = lax.broadcasted_iota(jnp.int32, (n_blk, tq), 0)
    t_row = t0 + lax.broadcasted_iota(jnp.int32, (n_blk, tq), 1)
    cur = t_row // SLC_BLOCK
    causal = blk <= cur
    forced = (blk == 0) | (blk == cur) | (blk == cur - 1)
    imp = jnp.where(forced, FORCE, imp)
    imp = jnp.where(causal, imp, NEG)
    rank = jnp.zeros((n_blk, tq), F32)
    for i in range(n_blk):
        r = imp[i:i + 1, :]
        rank = rank + jnp.where(blk > i, jnp.where(r >= imp, 1.0, 0.0), jnp.where(r > imp, 1.0, 0.0))
    bias_t = jnp.where(rank < float(min(N_SELECT, n_blk)), 0.0, NEG)
    pad_lo = jnp.zeros((NSA_DH, tq), F32)
    pieces = [pad_lo, bias_t]
    if n_blk < LANES - NSA_DH:
        pieces.append(jnp.full((LANES - NSA_DH - n_blk, tq), NEG, F32))
    bias_q = jnp.concatenate(pieces, axis=0).T
    for h in range(NSA_HPG):
        qaug_ref[0, h] = (q_pads[h] + bias_q).astype(BF16)


def _cmp_select(z, kvc, ovt, b, s, tq):
    t = b * s
    nq = s // tq
    bg = b * NSA_KV_HEADS
    ncp = kvc.shape[2]
    n_blk = s // SLC_BLOCK
    qcol = OFF_NQ // 256
    return pl.pallas_call(
        functools.partial(_cmp_select_kernel, tq=tq, n_blk=n_blk),
        grid=(bg, nq),
        in_specs=[
            pl.BlockSpec((tq, 256), lambda n, i: ((n // NSA_KV_HEADS) * nq + i, qcol + n % NSA_KV_HEADS)),
            pl.BlockSpec((1, 1, ncp, LANES), lambda n, i: (0, n, 0, 0)),
            pl.BlockSpec((1, 1, ncp, LANES), lambda n, i: (1, n, 0, 0)),
            pl.BlockSpec((n_blk, ncp), lambda n, i: (0, 0)),
        ],
        out_specs=[
            pl.BlockSpec((tq, 256), lambda n, i: ((n // NSA_KV_HEADS) * nq + i, n % NSA_KV_HEADS)),
            pl.BlockSpec((1, NSA_HPG, tq, LANES), lambda n, i: (n, 0, i, 0)),
        ],
        out_shape=[
            jax.ShapeDtypeStruct((t, D_MODEL), BF16),
            jax.ShapeDtypeStruct((bg, NSA_HPG, s, LANES), BF16),
        ],
        compiler_params=_cparams(("arbitrary", "arbitrary")),
        name="nsa_cmp_select",
    )(z, kvc, kvc, ovt)


def _normalised_heads(acc_of_head):
    outs = []
    for h in range(NSA_HPG):
        acc = acc_of_head(h)
        lane = lax.broadcasted_iota(jnp.int32, acc.shape, 1)
        denom = jnp.sum(jnp.where(lane == NSA_DH, acc, 0.0), axis=-1, keepdims=True)
        outs.append(jnp.where(lane < NSA_DH, acc * (1.0 / denom), 0.0))
    return _merge_heads(outs).astype(BF16)


def _flash_sel_kernel(q_ref, k_ref, v_ref, o_ref, m_scr, acc_scr, *, tq, tk):
    t0 = pl.program_id(1) * tq
    m_scr[...] = jnp.full(m_scr.shape, NEG, F32)
    acc_scr[...] = jnp.zeros(acc_scr.shape, F32)

    def tile(start, row0, masked):
        k = k_ref[0, pl.ds(start, tk), :]
        v = v_ref[0, pl.ds(start, tk), :]
        nr = tq - row0
        if masked:
            t_q = t0 + row0 + lax.broadcasted_iota(jnp.int32, (nr, tk), 0)
            ok = (start + lax.broadcasted_iota(jnp.int32, (nr, tk), 1)) <= t_q
        for h in range(NSA_HPG):
            sc = _nt_dot(q_ref[0, h, row0:tq, :], k)
            if masked:
                sc = jnp.where(ok, sc, NEG)
            m_prev = m_scr[h, row0:tq, :]
            m_new = jnp.maximum(m_prev, jnp.max(sc, axis=-1, keepdims=True))
            alpha = jnp.exp(m_prev - m_new)
            p = jnp.exp(sc - jnp.concatenate([m_new] * (tk // LANES), axis=1))
            acc_scr[h, row0:tq, :] = alpha * acc_scr[h, row0:tq, :] + _dot(p.astype(BF16), v)
            m_scr[h, row0:tq, :] = m_new

    def body(kt, carry):
        tile(pl.multiple_of(kt * tk, tk), 0, False)
        return carry

    lax.fori_loop(0, t0 // tk, body, 0)
    for j in range(tq // tk):
        tile(pl.multiple_of(t0 + j * tk, tk), j * tk, True)
    o_ref[...] = _normalised_heads(lambda h: acc_scr[h])


def _flash_win_kernel(q_ref, k_ref, v_ref, o_ref, *, tq):
    t0 = pl.program_id(1) * tq
    n_tiles = WINDOW // tq + 1
    ks, vs, biases = [], [], []
    for j in range(n_tiles):
        first = t0 + (j - n_tiles + 1) * tq
        start = pl.multiple_of(jnp.maximum(first, 0), tq)
        ks.append(k_ref[0, pl.ds(start, tq), :])
        vs.append(v_ref[0, pl.ds(start, tq), :])
        t_q = t0 + lax.broadcasted_iota(jnp.int32, (tq, tq), 0)
        kpos = first + lax.broadcasted_iota(jnp.int32, (tq, tq), 1)
        bias = jnp.where(kpos <= t_q, 0.0, NEG)
        bias = jnp.where(kpos > t_q - WINDOW, bias, NEG)
        biases.append(jnp.where(kpos >= 0, bias, NEG))

    def head(h):
        q = q_ref[0, h]
        scs = [_nt_dot(q, ks[j]) + biases[j] for j in range(n_tiles)]
        m = jnp.max(scs[0], axis=-1, keepdims=True)
        for sc in scs[1:]:
            m = jnp.maximum(m, jnp.max(sc, axis=-1, keepdims=True))
        acc = _dot(jnp.exp(scs[0] - m).astype(BF16), vs[0])
        for j in range(1, n_tiles):
            acc = acc + _dot(jnp.exp(scs[j] - m).astype(BF16), vs[j])
        return acc

    o_ref[...] = _normalised_heads(head)


def _flash(qaug, k_aug, v_aug, b, s, tq, tk, mode):
    t = b * s
    nq = s // tq
    bg = b * NSA_KV_HEADS
    if mode == "sel":
        body = functools.partial(_flash_sel_kernel, tq=tq, tk=tk)
        scratch = [pltpu.VMEM((NSA_HPG, tq, LANES), F32), pltpu.VMEM((NSA_HPG, tq, LANES), F32)]
    else:
        body = functools.partial(_flash_win_kernel, tq=tq)
        scratch = []
    return pl.pallas_call(
        body,
        grid=(bg, nq),
        in_specs=[
            pl.BlockSpec((1, NSA_HPG, tq, LANES), lambda n, i: (n, 0, i, 0)),
            pl.BlockSpec((1, s, LANES), lambda n, i: (n, 0, 0)),
            pl.BlockSpec((1, s, LANES), lambda n, i: (n, 0, 0)),
        ],
        out_specs=pl.BlockSpec((tq, 256), lambda n, i: ((n // NSA_KV_HEADS) * nq + i, n % NSA_KV_HEADS)),
        out_shape=jax.ShapeDtypeStruct((t, D_MODEL), BF16),
        scratch_shapes=scratch,
        compiler_params=_cparams(("arbitrary", "arbitrary")),
        name="nsa_flash_" + mode,
    )(qaug, k_aug, v_aug)


def _shifted(u, prev, n):
    rolled = pltpu.roll(u, n, axis=0)
    row = lax.broadcasted_iota(jnp.int32, u.shape, 0)
    out = rolled
    for r in range(n):
        out = jnp.where(row == r, prev[HALO_ROWS - n + r:HALO_ROWS - n + r + 1, :], out)
    return out


def _causal_conv3(u, prev, w):
    return w[0:1, :] * _shifted(u, prev, 2) + w[1:2, :] * _shifted(u, prev, 1) + w[2:3, :] * u


def _merge_kernel(x_ref, yret_ref, sb_ref, sc_ref, sx_ref, sch_ref, sxh_ref, mg0_ref, mg1_ref, mg2_ref,
                  ng_ref, ocmp_ref, oslc_ref, owin_ref, cw_ref, e_ref, wret_ref, wsc_ref, wnsa_ref,
                  wmix_ref, out_ref, *, tiles_per_seq):
    first = (pl.program_id(0) % tiles_per_seq) == 0
    u = sc_ref[...].astype(F32) * sx_ref[...].astype(F32)
    prev = sch_ref[...].astype(F32) * sxh_ref[...].astype(F32)
    prev = jnp.where(first, 0.0, prev)
    y_sc = (sb_ref[...].astype(F32) * _causal_conv3(u, prev, cw_ref[...])).astype(BF16)

    gates = _dot(_sigmoid(ng_ref[...].astype(F32)).astype(BF16), e_ref[...])
    y_nsa = (gates[:, 0:D_MODEL] * ocmp_ref[...].astype(F32)
             + gates[:, D_MODEL:2 * D_MODEL] * oslc_ref[...].astype(F32)
             + gates[:, 2 * D_MODEL:3 * D_MODEL] * owin_ref[...].astype(F32)).astype(BF16)

    merged = (_sigmoid(mg0_ref[...].astype(F32)) * _dot(yret_ref[...], wret_ref[...])
              + _sigmoid(mg1_ref[...].astype(F32)) * _dot(y_sc, wsc_ref[...])
              + _sigmoid(mg2_ref[...].astype(F32)) * _dot(y_nsa, wnsa_ref[...]))
    out_ref[...] = x_ref[...] + _dot(merged.astype(BF16), wmix_ref[...])


def _merge(x2, z, y_ret, o_cmp, o_slc, o_win, conv_w, gate_expand, w_ret, w_sc, w_nsa, w_mix, s, tm):
    t = x2.shape[0]
    row = lambda c: (lambda i: (i, c))
    halo = lambda c: (lambda i: (jnp.maximum(i * (tm // HALO_ROWS) - 1, 0), c))
    const = lambda i: (0, 0)
    wspec = pl.BlockSpec((D_MODEL, D_MODEL), const)
    return pl.pallas_call(
        functools.partial(_merge_kernel, tiles_per_seq=s // tm),
        grid=(t // tm,),
        in_specs=[
            pl.BlockSpec((tm, D_MODEL), row(0)),
            pl.BlockSpec((tm, D_MODEL), row(0)),
            pl.BlockSpec((tm, D_MODEL), row(OFF_SB // D_MODEL)),
            pl.BlockSpec((tm, D_MODEL), row(OFF_SC // D_MODEL)),
            pl.BlockSpec((tm, D_MODEL), row(OFF_SX // D_MODEL)),
            pl.BlockSpec((HALO_ROWS, D_MODEL), halo(OFF_SC // D_MODEL)),
            pl.BlockSpec((HALO_ROWS, D_MODEL), halo(OFF_SX // D_MODEL)),
            pl.BlockSpec((tm, D_MODEL), row(OFF_MG // D_MODEL)),
            pl.BlockSpec((tm, D_MODEL), row(OFF_MG // D_MODEL + 1)),
            pl.BlockSpec((tm, D_MODEL), row(OFF_MG // D_MODEL + 2)),
            pl.BlockSpec((tm, LANES), row(OFF_NG // LANES)),
            pl.BlockSpec((tm, D_MODEL), row(0)),
            pl.BlockSpec((tm, D_MODEL), row(0)),
            pl.BlockSpec((tm, D_MODEL), row(0)),
            pl.BlockSpec((CONV_WIDTH, D_MODEL), const),
            pl.BlockSpec((LANES, 3 * D_MODEL), const),
            wspec, wspec, wspec, wspec,
        ],
        out_specs=pl.BlockSpec((tm, D_MODEL), row(0)),
        out_shape=jax.ShapeDtypeStruct((t, D_MODEL), F32),
        compiler_params=_cparams(("arbitrary",)),
        name="merge",
    )(x2, y_ret, z, z, z, z, z, z, z, z, z, o_cmp, o_slc, o_win, conv_w, gate_expand,
      w_ret, w_sc, w_nsa, w_mix)


def _ffn_kernel(x_ref, nw_ref, wa_ref, wv_ref, cwa_ref, cwv_ref, wd_ref, fw_ref, out_ref,
                h_scr, acc_scr, ca_scr, cv_scr, *, tiles_per_seq, final_norm):
    i = pl.program_id(0)
    j = pl.program_id(1)
    first = (i % tiles_per_seq) == 0

    @pl.when(j == 0)
    def _():
        h_scr[...] = _rms(x_ref[...], nw_ref[...]).astype(BF16)
        acc_scr[...] = jnp.zeros_like(acc_scr)

    h = h_scr[...]
    tm = h.shape[0]
    ua = _dot(h, wa_ref[...])
    uv = _dot(h, wv_ref[...])
    @pl.when(first)
    def _():
        ca_scr[j] = jnp.zeros(ca_scr.shape[1:], F32)
        cv_scr[j] = jnp.zeros(cv_scr.shape[1:], F32)

    prev_a = ca_scr[j]
    prev_v = cv_scr[j]
    ca_scr[j] = ua[tm - HALO_ROWS:, :]
    cv_scr[j] = uv[tm - HALO_ROWS:, :]
    a = _causal_conv3(ua, prev_a, cwa_ref[...])
    v = _causal_conv3(uv, prev_v, cwv_ref[...])
    act = (a * _sigmoid(a) * v).astype(BF16)
    acc_scr[...] += _dot(act, wd_ref[...])

    @pl.when(j == pl.num_programs(1) - 1)
    def _():
        y = x_ref[...] + acc_scr[...]
        if final_norm:
            y = _rms(y, fw_ref[...])
        out_ref[...] = y


def _ffn(x2, norm_w, w_up, conv_w, w_down, final_w, s, tm, tf, final_norm):
    t = x2.shape[0]
    nf = D_FF // tf
    return pl.pallas_call(
        functools.partial(_ffn_kernel, tiles_per_seq=s // tm, final_norm=final_norm),
        grid=(t // tm, nf),
        in_specs=[
            pl.BlockSpec((tm, D_MODEL), lambda i, j: (i, 0)),
            pl.BlockSpec((1, D_MODEL), lambda i, j: (0, 0)),
            pl.BlockSpec((D_MODEL, tf), lambda i, j: (0, j)),
            pl.BlockSpec((D_MODEL, tf), lambda i, j: (0, nf + j)),
            pl.BlockSpec((CONV_WIDTH, tf), lambda i, j: (0, j)),
            pl.BlockSpec((CONV_WIDTH, tf), lambda i, j: (0, nf + j)),
            pl.BlockSpec((tf, D_MODEL), lambda i, j: (j, 0)),
            pl.BlockSpec((1, D_MODEL), lambda i, j: (0, 0)),
        ],
        out_specs=pl.BlockSpec((tm, D_MODEL), lambda i, j: (i, 0)),
        out_shape=jax.ShapeDtypeStruct((t, D_MODEL), F32),
        scratch_shapes=[
            pltpu.VMEM((tm, D_MODEL), BF16),
            pltpu.VMEM((tm, D_MODEL), F32),
            pltpu.VMEM((nf, HALO_ROWS, tf), F32),
            pltpu.VMEM((nf, HALO_ROWS, tf), F32),
        ],
        compiler_params=_cparams(("arbitrary", "arbitrary")),
        name="ffn",
    )(x2, norm_w, w_up, w_up, conv_w, conv_w, w_down, final_w)


def _prep_w_in(w_in_l):
    n_kv = 6 * NSA_KV_HEADS * NSA_DH
    nkv = w_in_l[:, OFF_MG:OFF_MG + n_kv]
    ng = w_in_l[:, OFF_MG + n_kv:OFF_MG + n_kv + 3 * NSA_HEADS]
    ng = ng.reshape(D_MODEL, NSA_HEADS, 3).transpose(0, 2, 1).reshape(D_MODEL, 3 * NSA_HEADS)
    mg = w_in_l[:, OFF_MG + n_kv + 3 * NSA_HEADS:]
    pad = jnp.zeros((D_MODEL, Z_WIDTH - OFF_NG - 3 * NSA_HEADS), w_in_l.dtype)
    return jnp.concatenate([w_in_l[:, :OFF_MG], mg, nkv, ng, pad], axis=1).astype(BF16)


def _gate_expander():
    e = np.zeros((LANES, 3 * D_MODEL), np.float32)
    for br in range(3):
        for h in range(NSA_HEADS):
            e[br * NSA_HEADS + h, br * D_MODEL + h * NSA_DH: br * D_MODEL + (h + 1) * NSA_DH] = 1.0
    return jnp.asarray(e, BF16)


def _overlap_t(s, ncp):
    nc = (s - CMP_BLOCK) // CMP_STRIDE + 1
    ns = s // SLC_BLOCK
    ci = np.arange(ncp) * CMP_STRIDE
    sj = np.arange(ns) * SLC_BLOCK
    ov = ((ci[None, :] < sj[:, None] + SLC_BLOCK) & (ci[None, :] + CMP_BLOCK > sj[:, None])
          & (np.arange(ncp)[None, :] < nc))
    return jnp.asarray(ov.astype(np.float32), BF16)


def _group_major(a, b, s):
    return a.reshape(b, s, NSA_KV_HEADS, NSA_DH).transpose(0, 2, 1, 3).reshape(b * NSA_KV_HEADS, s, NSA_DH)


def kernel(x, attn_norm_w, w_in, ret_norm_w, w_ret_out, sc_conv_w, w_sc_out, nsa_cmp_pos, nsa_cmp_w1,
           nsa_cmp_w2, w_nsa_out, w_mix_out, ffn_norm_w, w_ffn_up, ffn_conv_w, w_ffn_down, final_norm_w):
    b, s, _ = x.shape
    t = b * s
    depth = w_in.shape[0]
    bg = b * NSA_KV_HEADS
    ncp = s // CMP_STRIDE
    assert s % 256 == 0 and s // SLC_BLOCK <= NSA_DH

    tm_in = min(1024, t)
    tm_merge = 256
    tm_ffn = min(1024, s)
    tq_cmp = 128
    tq_sel = 512
    tk_sel = 256
    tq_win = 256
    tr = 256

    ret_consts = _retention_consts(s)
    gate_expand = _gate_expander()
    ovt = _overlap_t(s, ncp)
    blk_onehot = jnp.asarray(np.eye(NSA_DH, dtype=np.float32)[np.arange(s) // SLC_BLOCK], BF16)
    ones_col = jnp.zeros((s, NSA_DH), BF16).at[:, 0].set(1.0)
    zeros_col = jnp.zeros((s, NSA_DH), BF16)
    bcast = lambda c: jnp.broadcast_to(c[None], (bg, s, NSA_DH))

    x2 = x.reshape(t, D_MODEL)
    for l in range(depth):
        z = _inproj(x2, attn_norm_w[l][None], _prep_w_in(w_in[l]), tm_in, 1024)
        y_ret = _retention(z, ret_norm_w[l][None], ret_consts, b, s, tr)

        kv = [_group_major(z[:, OFF_NKV + 256 * n: OFF_NKV + 256 * (n + 1)], b, s) for n in range(6)]
        k_c, v_c, k_s, v_s, k_w, v_w = kv

        def blocks(a):
            r = a.reshape(bg, ncp, CMP_STRIDE * NSA_DH)
            nxt = jnp.concatenate([r[:, 1:], jnp.zeros_like(r[:, :1])], axis=1)
            return jnp.concatenate([r, nxt], axis=-1)

        blk = jnp.stack([blocks(k_c), blocks(v_c)])
        pos_flat = nsa_cmp_pos[l].reshape(2, 1, CMP_BLOCK * NSA_DH)
        w2p = jnp.pad(nsa_cmp_w2[l], ((0, 0), (0, 0), (0, LANES - NSA_DH))).astype(BF16)
        kvc = _compress(blk, pos_flat, nsa_cmp_w1[l].astype(BF16), w2p)

        o_cmp, qaug = _cmp_select(z, kvc, ovt, b, s, tq_cmp)
        ks_aug = jnp.concatenate([k_s, bcast(blk_onehot)], axis=-1)
        vs_aug = jnp.concatenate([v_s, bcast(ones_col)], axis=-1)
        kw_aug = jnp.concatenate([k_w, bcast(zeros_col)], axis=-1)
        vw_aug = jnp.concatenate([v_w, bcast(ones_col)], axis=-1)
        o_slc = _flash(qaug, ks_aug, vs_aug, b, s, tq_sel, tk_sel, "sel")
        o_win = _flash(qaug, kw_aug, vw_aug, b, s, tq_win, None, "win")

        x2 = _merge(x2, z, y_ret, o_cmp, o_slc, o_win, sc_conv_w[l], gate_expand,
                    w_ret_out[l].astype(BF16), w_sc_out[l].astype(BF16), w_nsa_out[l].astype(BF16),
                    w_mix_out[l].astype(BF16), s, tm_merge)
        x2 = _ffn(x2, ffn_norm_w[l][None], w_ffn_up[l].astype(BF16), ffn_conv_w[l],
                  w_ffn_down[l].astype(BF16), final_norm_w[None], s, tm_ffn, 256, l == depth - 1)
    return x2.reshape(b, s, D_MODEL)
```

```python
import functools

import numpy as np
import jax
import jax.numpy as jnp
from jax import lax
from jax.experimental import pallas as pl
from jax.experimental.pallas import tpu as pltpu

F32 = jnp.float32
BF16 = jnp.bfloat16

D_MODEL = 1024
RET_HEADS = 4
RET_DV = D_MODEL // RET_HEADS
RET_DK = RET_DV // 2
RET_CHUNK = 128
CONV_WIDTH = 3
NSA_DH = 64
NSA_HEADS = D_MODEL // NSA_DH
NSA_KV_HEADS = 4
NSA_HPG = NSA_HEADS // NSA_KV_HEADS
CMP_BLOCK = 32
CMP_STRIDE = 16
CMP_HIDDEN = 256
SLC_BLOCK = 64
N_SELECT = 16
WINDOW = 512
D_FF = ((8 * D_MODEL // 3 + 127) // 128) * 128
EPS = 1e-6
NEG = -1e30
FORCE = 1e6

FFN_ROW_GROUPS = 1
LANES = 128
HALO_ROWS = 8

OFF_RQ = 0
OFF_RK = 512
OFF_RV = 1024
OFF_RG = 2048
OFF_SB = 3072
OFF_SC = 4096
OFF_SX = 5120
OFF_NQ = 6144
OFF_MG = 7168
OFF_KC = 10240
OFF_VC = 10496
OFF_KS = 10752
OFF_KW = 11008
OFF_VS = 11264
OFF_VW = 11776
OFF_NG = 12288
Z_WIDTH = 12800
Z_TILE = 1280

VMEM_LIMIT = 56 * 1024 * 1024


def _cparams(sem):
    return pltpu.CompilerParams(dimension_semantics=sem, vmem_limit_bytes=VMEM_LIMIT)


def _nt_dot(a, b):
    return lax.dot_general(a, b, (((1,), (1,)), ((), ())), preferred_element_type=F32)


def _dot(a, b):
    return jnp.dot(a, b, preferred_element_type=F32)


def _sigmoid(x):
    return 1.0 / (1.0 + jnp.exp(-x))


def _rms(x, w):
    return x * lax.rsqrt(jnp.mean(x * x, axis=-1, keepdims=True) + EPS) * w


def _inproj_kernel(x_ref, nw_ref, w_ref, b_ref, z_ref, h_scr):
    @pl.when(pl.program_id(1) == 0)
    def _():
        h_scr[...] = _rms(x_ref[...], nw_ref[...]).astype(BF16)

    z_ref[...] = (_dot(h_scr[...], w_ref[...]) + b_ref[...]).astype(BF16)


def _inproj(x2, norm_w, w_in_p, z_bias, tm, tn):
    t = x2.shape[0]
    return pl.pallas_call(
        _inproj_kernel,
        grid=(t // tm, Z_WIDTH // tn),
        in_specs=[
            pl.BlockSpec((tm, D_MODEL), lambda i, j: (i, 0)),
            pl.BlockSpec((1, D_MODEL), lambda i, j: (0, 0)),
            pl.BlockSpec((D_MODEL, tn), lambda i, j: (0, j)),
            pl.BlockSpec((1, tn), lambda i, j: (0, j)),
        ],
        out_specs=pl.BlockSpec((tm, tn), lambda i, j: (i, j)),
        out_shape=jax.ShapeDtypeStruct((t, Z_WIDTH), BF16),
        scratch_shapes=[pltpu.VMEM((tm, D_MODEL), BF16)],
        compiler_params=_cparams(("arbitrary", "arbitrary")),
        name="inproj",
    )(x2, norm_w, w_in_p, z_bias)


def _ret_kernel(q_ref, k_ref, v_ref, g_ref, cos_ref, sin_ref, dm_ref, zeta_ref, xi_ref,
                dec_ref, nw_ref, y_ref, st_scr, *, n_chunks):
    @pl.when(pl.program_id(1) == 0)
    def _():
        st_scr[...] = jnp.zeros_like(st_scr)

    c = RET_CHUNK
    for ci in range(n_chunks):
        rows = slice(ci * c, (ci + 1) * c)
        cos = cos_ref[rows, :]
        sin = sin_ref[rows, :]
        for h in range(RET_HEADS):
            q = q_ref[rows, h * RET_DK:(h + 1) * RET_DK].astype(F32)
            k = k_ref[rows, h * RET_DK:(h + 1) * RET_DK].astype(F32)
            qr = q * cos + pltpu.roll(q, RET_DK // 2, axis=1) * sin
            kr = (k * cos + pltpu.roll(k, RET_DK // 2, axis=1) * sin) * (RET_DK ** -0.5)
            qb = qr.astype(BF16)
            kb = kr.astype(BF16)
            v = v_ref[rows, h * RET_DV:(h + 1) * RET_DV]
            scores = _nt_dot(qb, kb) * dm_ref[h]
            o = _dot(scores.astype(BF16), v)
            st = st_scr[h]
            o = o + _dot(qb, st.astype(BF16)) * xi_ref[h]
            vz = (v.astype(F32) * zeta_ref[h]).astype(BF16)
            kv = _dot(kr.T.astype(BF16), vz)
            st_scr[h] = st * dec_ref[h] + kv
            mu = jnp.mean(o, axis=-1, keepdims=True)
            d = o - mu
            var = jnp.mean(d * d, axis=-1, keepdims=True)
            on = d * lax.rsqrt(var + EPS) * nw_ref[:, h * RET_DV:(h + 1) * RET_DV]
            g = g_ref[rows, h * RET_DV:(h + 1) * RET_DV].astype(F32)
            y_ref[rows, h * RET_DV:(h + 1) * RET_DV] = (g * _sigmoid(g) * on).astype(BF16)


def _retention_consts(s):
    c = RET_CHUNK
    pos = jnp.arange(s, dtype=F32)
    theta = 10000.0 ** (-jnp.linspace(0.0, 1.0, RET_DK // 2, dtype=F32))
    ang = pos[:, None] * theta[None, :]
    cos = jnp.cos(ang)
    sin = jnp.sin(ang)
    cos2 = jnp.concatenate([cos, cos], axis=-1)
    sin2 = jnp.concatenate([-sin, sin], axis=-1)
    log_gamma = jnp.log1p(-(2.0 ** (-5.0 - jnp.arange(RET_HEADS, dtype=F32))))
    j = jnp.arange(c, dtype=F32)
    rel = j[:, None] - j[None, :]
    dmask = jnp.where(rel >= 0, jnp.exp(log_gamma[:, None, None] * jnp.maximum(rel, 0.0)), 0.0)
    zeta = jnp.exp(log_gamma[:, None] * (c - 1 - j)[None, :])
    xi = jnp.exp(log_gamma[:, None] * (j + 1.0)[None, :])
    dec = jnp.exp(log_gamma * c)
    zeta_b = jnp.broadcast_to(zeta[:, :, None], (RET_HEADS, c, RET_DV))
    xi_b = jnp.broadcast_to(xi[:, :, None], (RET_HEADS, c, RET_DV))
    dec_b = jnp.broadcast_to(dec[:, None, None], (RET_HEADS, RET_DK, RET_DV))
    return cos2, sin2, dmask, zeta_b, xi_b, dec_b


def _retention(z, ret_norm_w, consts, b, s, tr):
    t = b * s
    nt = s // tr
    cos2, sin2, dmask, zeta_b, xi_b, dec_b = consts
    const3 = lambda bi, i: (0, 0, 0)
    return pl.pallas_call(
        functools.partial(_ret_kernel, n_chunks=tr // RET_CHUNK),
        grid=(b, nt),
        in_specs=[
            pl.BlockSpec((tr, 512), lambda bi, i: (bi * nt + i, OFF_RQ // 512)),
            pl.BlockSpec((tr, 512), lambda bi, i: (bi * nt + i, OFF_RK // 512)),
            pl.BlockSpec((tr, 1024), lambda bi, i: (bi * nt + i, OFF_RV // 1024)),
            pl.BlockSpec((tr, 1024), lambda bi, i: (bi * nt + i, OFF_RG // 1024)),
            pl.BlockSpec((tr, RET_DK), lambda bi, i: (i, 0)),
            pl.BlockSpec((tr, RET_DK), lambda bi, i: (i, 0)),
            pl.BlockSpec((RET_HEADS, RET_CHUNK, RET_CHUNK), const3),
            pl.BlockSpec((RET_HEADS, RET_CHUNK, RET_DV), const3),
            pl.BlockSpec((RET_HEADS, RET_CHUNK, RET_DV), const3),
            pl.BlockSpec((RET_HEADS, RET_DK, RET_DV), const3),
            pl.BlockSpec((1, D_MODEL), lambda bi, i: (0, 0)),
        ],
        out_specs=pl.BlockSpec((tr, D_MODEL), lambda bi, i: (bi * nt + i, 0)),
        out_shape=jax.ShapeDtypeStruct((t, D_MODEL), BF16),
        scratch_shapes=[pltpu.VMEM((RET_HEADS, RET_DK, RET_DV), F32)],
        compiler_params=_cparams(("arbitrary", "arbitrary")),
        name="retention",
    )(z, z, z, z, cos2, sin2, dmask, zeta_b, xi_b, dec_b, ret_norm_w)


def _compress_kernel(blk_ref, pos_ref, w1_ref, w2_ref, out_ref):
    blk = (blk_ref[0, 0].astype(F32) + pos_ref[0]).astype(BF16)
    hid = _dot(blk, w1_ref[0])
    gel = 0.5 * hid * (1.0 + jnp.tanh(np.sqrt(2.0 / np.pi) * (hid + 0.044715 * (hid * hid * hid))))
    out_ref[0, 0] = _dot(gel.astype(BF16), w2_ref[0]).astype(BF16)


def _compress(blk, pos_flat, w1, w2p):
    _, bg, ncp, width = blk.shape
    return pl.pallas_call(
        _compress_kernel,
        grid=(2, bg),
        in_specs=[
            pl.BlockSpec((1, 1, ncp, width), lambda a, n: (a, n, 0, 0)),
            pl.BlockSpec((1, 1, width), lambda a, n: (a, 0, 0)),
            pl.BlockSpec((1, width, CMP_HIDDEN), lambda a, n: (a, 0, 0)),
            pl.BlockSpec((1, CMP_HIDDEN, LANES), lambda a, n: (a, 0, 0)),
        ],
        out_specs=pl.BlockSpec((1, 1, ncp, LANES), lambda a, n: (a, n, 0, 0)),
        out_shape=jax.ShapeDtypeStruct((2, bg, ncp, LANES), BF16),
        compiler_params=_cparams(("arbitrary", "arbitrary")),
        name="nsa_compress",
    )(blk, pos_flat, w1, w2p)


def _head_pad(q2_ref_block, h):
    x = q2_ref_block[:, (h // 2) * LANES:(h // 2 + 1) * LANES].astype(F32)
    if h % 2:
        x = pltpu.roll(x, NSA_DH, axis=1)
    lane = lax.broadcasted_iota(jnp.int32, x.shape, 1)
    return jnp.where(lane < NSA_DH, x, 0.0)


def _merge_heads(parts):
    lo = parts[0] + pltpu.roll(parts[1], NSA_DH, axis=1)
    hi = parts[2] + pltpu.roll(parts[3], NSA_DH, axis=1)
    return jnp.concatenate([lo, hi], axis=1)


def _cmp_select_kernel(q_ref, kc_ref, vc_ref, ovt_ref, ocmp_ref, bias_ref, *, tq, n_blk):
    t0 = pl.program_id(1) * tq
    ncp = kc_ref.shape[2]
    kc = kc_ref[0, 0]
    vc = vc_ref[0, 0]
    ovt = ovt_ref[...]
    scale = NSA_DH ** -0.5

    t_q = t0 + lax.broadcasted_iota(jnp.int32, (tq, ncp), 0)
    c_id = lax.broadcasted_iota(jnp.int32, (tq, ncp), 1)
    valid = (c_id * CMP_STRIDE + (CMP_BLOCK - 1)) <= t_q
    t_col = t0 + lax.broadcasted_iota(jnp.int32, (tq, 1), 0)
    any1 = jnp.where(t_col >= CMP_BLOCK - 1, 1.0, 0.0)

    imp = jnp.zeros((n_blk, tq), F32)
    o_parts = []
    for h in range(NSA_HPG):
        qp = (_head_pad(q_ref, h) * scale).astype(BF16)
        s1 = jnp.where(valid, _nt_dot(qp, kc), NEG)
        m = jnp.max(s1, axis=-1, keepdims=True)
        e = jnp.exp(s1 - m)
        pb = (e * ((1.0 / jnp.sum(e, axis=-1, keepdims=True)) * any1)).astype(BF16)
        o_parts.append(_dot(pb, vc))
        imp = imp + _nt_dot(ovt, pb)
    ocmp_ref[...] = _merge_heads(o_parts).astype(BF16)

    blk = lax.broadcasted_iota(jnp.int32, (n_blk, tq), 0)
    t_row = t0 + lax.broadcasted_iota(jnp.int32, (n_blk, tq), 1)
    cur = t_row // SLC_BLOCK
    causal = blk <= cur
    forced = (blk == 0) | (blk == cur) | (blk == cur - 1)
    imp = jnp.where(forced, FORCE, imp)
    imp = jnp.where(causal, imp, NEG)
    sub = 8
    groups = [imp[g * sub:(g + 1) * sub, :] for g in range(n_blk // sub)]
    ranks = [jnp.zeros((sub, tq), F32) for _ in groups]
    row_in_group = lax.broadcasted_iota(jnp.int32, (sub, tq), 0)
    for i in range(n_blk):
        r = jnp.broadcast_to(imp[i:i + 1, :], (sub, tq))
        for g, x in enumerate(groups):
            if g < i // sub:
                ahead = jnp.where(r > x, 1.0, 0.0)
            elif g > i // sub:
                ahead = jnp.where(r >= x, 1.0, 0.0)
            else:
                ahead = jnp.where(row_in_group > i % sub, jnp.where(r >= x, 1.0, 0.0), jnp.where(r > x, 1.0, 0.0))
            ranks[g] = ranks[g] + ahead
    rank = jnp.concatenate(ranks, axis=0)
    bias_t = jnp.where(rank < float(min(N_SELECT, n_blk)), 0.0, NEG)
    pieces = [bias_t]
    if n_blk < NSA_DH:
        pieces.append(jnp.full((NSA_DH - n_blk, tq), NEG, F32))
    pieces.append(jnp.zeros((LANES - NSA_DH, tq), F32))
    bias_ref[0] = jnp.concatenate(pieces, axis=0).T.astype(BF16)


def _cmp_select(z, kvc, ovt, b, s, tq):
    t = b * s
    nq = s // tq
    bg = b * NSA_KV_HEADS
    ncp = kvc.shape[2]
    n_blk = s // SLC_BLOCK
    qcol = OFF_NQ // 256
    return pl.pallas_call(
        functools.partial(_cmp_select_kernel, tq=tq, n_blk=n_blk),
        grid=(bg, nq),
        in_specs=[
            pl.BlockSpec((tq, 256), lambda n, i: ((n // NSA_KV_HEADS) * nq + i, qcol + n % NSA_KV_HEADS)),
            pl.BlockSpec((1, 1, ncp, LANES), lambda n, i: (0, n, 0, 0)),
            pl.BlockSpec((1, 1, ncp, LANES), lambda n, i: (1, n, 0, 0)),
            pl.BlockSpec((n_blk, ncp), lambda n, i: (0, 0)),
        ],
        out_specs=[
            pl.BlockSpec((tq, 256), lambda n, i: ((n // NSA_KV_HEADS) * nq + i, n % NSA_KV_HEADS)),
            pl.BlockSpec((1, tq, LANES), lambda n, i: (n, i, 0)),
        ],
        out_shape=[
            jax.ShapeDtypeStruct((t, D_MODEL), BF16),
            jax.ShapeDtypeStruct((bg, s, LANES), BF16),
        ],
        compiler_params=_cparams(("arbitrary", "arbitrary")),
        name="nsa_cmp_select",
    )(z, kvc, kvc, ovt)


def _normalised_heads(acc_of_head):
    outs = []
    for h in range(NSA_HPG):
        acc = acc_of_head(h)
        lane = lax.broadcasted_iota(jnp.int32, acc.shape, 1)
        denom = jnp.sum(jnp.where(lane == NSA_DH, acc, 0.0), axis=-1, keepdims=True)
        outs.append(jnp.where(lane < NSA_DH, acc * (1.0 / denom), 0.0))
    return _merge_heads(outs).astype(BF16)


def _q_pair(q_ref, h, odd_group):
    x = q_ref[:, (h // 2) * LANES:(h // 2 + 1) * LANES].astype(F32) * (NSA_DH ** -0.5)
    odd_head = jnp.bool_(h % 2 == 1)
    x = jnp.where(odd_head != odd_group, pltpu.roll(x, NSA_DH, axis=1), x)
    lane_half = lax.broadcasted_iota(jnp.int32, x.shape, 1) // NSA_DH
    return jnp.where(lane_half == odd_group.astype(jnp.int32), x, 0.0).astype(BF16)


def _flash_sel_kernel(q_ref, b_ref, k_ref, oh_ref, v_ref, o_ref, qa_scr, m_scr, acc_scr, *, tq, tk, td):
    t0 = pl.program_id(1) * tq
    odd_group = (pl.program_id(0) % 2) == 1
    for h in range(NSA_HPG):
        qa_scr[h] = jnp.concatenate([_q_pair(q_ref, h, odd_group), b_ref[0]], axis=1)
    m_scr[...] = jnp.full(m_scr.shape, NEG, F32)
    acc_scr[...] = jnp.zeros(acc_scr.shape, F32)

    def tile(start, row0, width, masked):
        k = jnp.concatenate([k_ref[pl.ds(start, width), :], oh_ref[pl.ds(start, width), :]], axis=1)
        v = v_ref[pl.ds(start, width), :]
        nr = tq - row0
        if masked:
            t_q = t0 + row0 + lax.broadcasted_iota(jnp.int32, (nr, width), 0)
            ok = (start + lax.broadcasted_iota(jnp.int32, (nr, width), 1)) <= t_q
        for h in range(NSA_HPG):
            sc = _nt_dot(qa_scr[h, row0:tq, :], k)
            if masked:
                sc = jnp.where(ok, sc, NEG)
            m_prev = m_scr[h, row0:tq, :]
            m_new = jnp.maximum(m_prev, jnp.max(sc, axis=-1, keepdims=True))
            alpha = jnp.exp(m_prev - m_new)
            p = jnp.exp(sc - jnp.concatenate([m_new] * (width // LANES), axis=1))
            acc_scr[h, row0:tq, :] = alpha * acc_scr[h, row0:tq, :] + _dot(p.astype(BF16), v)
            m_scr[h, row0:tq, :] = m_new

    def body(kt, carry):
        tile(pl.multiple_of(kt * tk, tk), 0, tk, False)
        return carry

    lax.fori_loop(0, t0 // tk, body, 0)
    for j in range(tq // td):
        tile(pl.multiple_of(t0 + j * td, td), j * td, td, True)
    o_ref[...] = _normalised_heads(lambda h: acc_scr[h])


def _flash_win_kernel(q_ref, k_ref, v_ref, o_ref, *, tq):
    t0 = pl.program_id(1) * tq
    odd_group = (pl.program_id(0) % 2) == 1
    n_tiles = WINDOW // tq + 1
    ks, vs, biases = [], [], []
    for j in range(n_tiles):
        first = t0 + (j - n_tiles + 1) * tq
        start = pl.multiple_of(jnp.maximum(first, 0), tq)
        ks.append(k_ref[pl.ds(start, tq), :])
        vs.append(v_ref[pl.ds(start, tq), :])
        t_q = t0 + lax.broadcasted_iota(jnp.int32, (tq, tq), 0)
        kpos = first + lax.broadcasted_iota(jnp.int32, (tq, tq), 1)
        bias = jnp.where(kpos <= t_q, 0.0, NEG)
        bias = jnp.where(kpos > t_q - WINDOW, bias, NEG)
        biases.append(jnp.where(kpos >= 0, bias, NEG))

    def head(h):
        q = _q_pair(q_ref, h, odd_group)
        scs = [_nt_dot(q, ks[j]) + biases[j] for j in range(n_tiles)]
        m = jnp.max(scs[0], axis=-1, keepdims=True)
        for sc in scs[1:]:
            m = jnp.maximum(m, jnp.max(sc, axis=-1, keepdims=True))
        acc = _dot(jnp.exp(scs[0] - m).astype(BF16), vs[0])
        for j in range(1, n_tiles):
            acc = acc + _dot(jnp.exp(scs[j] - m).astype(BF16), vs[j])
        return acc

    o_ref[...] = _normalised_heads(head)


def _flash(z, sel_bias, blk_onehot, b, s, tq, tk, td, mode):
    t = b * s
    nq = s // tq
    bg = b * NSA_KV_HEADS
    grp = lambda n: n % NSA_KV_HEADS
    bat = lambda n: n // NSA_KV_HEADS
    off_k, off_v = (OFF_KS, OFF_VS) if mode == "sel" else (OFF_KW, OFF_VW)
    q_spec = pl.BlockSpec((tq, 256), lambda n, i: (bat(n) * nq + i, OFF_NQ // 256 + grp(n)))
    k_spec = pl.BlockSpec((s, LANES), lambda n, i: (bat(n), off_k // LANES + grp(n) // 2))
    v_spec = pl.BlockSpec((s, LANES), lambda n, i: (bat(n), off_v // LANES + grp(n)))
    if mode == "sel":
        body = functools.partial(_flash_sel_kernel, tq=tq, tk=tk, td=td)
        in_specs = [q_spec, pl.BlockSpec((1, tq, LANES), lambda n, i: (n, i, 0)), k_spec,
                    pl.BlockSpec((s, LANES), lambda n, i: (0, 0)), v_spec]
        args = (z, sel_bias, z, blk_onehot, z)
        scratch = [pltpu.VMEM((NSA_HPG, tq, 2 * LANES), BF16), pltpu.VMEM((NSA_HPG, tq, LANES), F32),
                   pltpu.VMEM((NSA_HPG, tq, LANES), F32)]
    else:
        body = functools.partial(_flash_win_kernel, tq=tq)
        in_specs = [q_spec, k_spec, v_spec]
        args = (z, z, z)
        scratch = []
    return pl.pallas_call(
        body,
        grid=(bg, nq),
        in_specs=in_specs,
        out_specs=pl.BlockSpec((tq, 256), lambda n, i: (bat(n) * nq + i, grp(n))),
        out_shape=jax.ShapeDtypeStruct((t, D_MODEL), BF16),
        scratch_shapes=scratch,
        compiler_params=_cparams(("arbitrary", "arbitrary")),
        name="nsa_flash_" + mode,
    )(*args)


def _shifted(u, prev, n):
    rolled = pltpu.roll(u, n, axis=0)
    head = rolled[:HALO_ROWS]
    row = lax.broadcasted_iota(jnp.int32, head.shape, 0)
    for r in range(n):
        head = jnp.where(row == r, prev[HALO_ROWS - n + r:HALO_ROWS - n + r + 1, :], head)
    return jnp.concatenate([head, rolled[HALO_ROWS:]], axis=0)


def _causal_conv3(u, prev, w):
    return w[0:1, :] * _shifted(u, prev, 2) + w[1:2, :] * _shifted(u, prev, 1) + w[2:3, :] * u


def _merge_kernel(x_ref, yret_ref, sb_ref, sc_ref, sx_ref, sch_ref, sxh_ref, mg0_ref, mg1_ref, mg2_ref,
                  ng_ref, ocmp_ref, oslc_ref, owin_ref, cw_ref, e_ref, wret_ref, wsc_ref, wnsa_ref,
                  wmix_ref, out_ref, *, tiles_per_seq):
    first = (pl.program_id(0) % tiles_per_seq) == 0
    u = sc_ref[...].astype(F32) * sx_ref[...].astype(F32)
    prev = sch_ref[...].astype(F32) * sxh_ref[...].astype(F32)
    prev = jnp.where(first, 0.0, prev)
    y_sc = (sb_ref[...].astype(F32) * _causal_conv3(u, prev, cw_ref[...])).astype(BF16)

    gates = _dot(_sigmoid(ng_ref[...].astype(F32)).astype(BF16), e_ref[...])
    y_nsa = (gates[:, 0:D_MODEL] * ocmp_ref[...].astype(F32)
             + gates[:, D_MODEL:2 * D_MODEL] * oslc_ref[...].astype(F32)
             + gates[:, 2 * D_MODEL:3 * D_MODEL] * owin_ref[...].astype(F32)).astype(BF16)

    merged = (_sigmoid(mg0_ref[...].astype(F32)) * _dot(yret_ref[...], wret_ref[...])
              + _sigmoid(mg1_ref[...].astype(F32)) * _dot(y_sc, wsc_ref[...])
              + _sigmoid(mg2_ref[...].astype(F32)) * _dot(y_nsa, wnsa_ref[...]))
    out_ref[...] = x_ref[...] + _dot(merged.astype(BF16), wmix_ref[...])


def _merge(x2, z, y_ret, o_cmp, o_slc, o_win, conv_w, gate_expand, w_ret, w_sc, w_nsa, w_mix, s, tm):
    t = x2.shape[0]
    row = lambda c: (lambda i: (i, c))
    halo = lambda c: (lambda i: (jnp.maximum(i * (tm // HALO_ROWS) - 1, 0), c))
    const = lambda i: (0, 0)
    wspec = pl.BlockSpec((D_MODEL, D_MODEL), const)
    return pl.pallas_call(
        functools.partial(_merge_kernel, tiles_per_seq=s // tm),
        grid=(t // tm,),
        in_specs=[
            pl.BlockSpec((tm, D_MODEL), row(0)),
            pl.BlockSpec((tm, D_MODEL), row(0)),
            pl.BlockSpec((tm, D_MODEL), row(OFF_SB // D_MODEL)),
            pl.BlockSpec((tm, D_MODEL), row(OFF_SC // D_MODEL)),
            pl.BlockSpec((tm, D_MODEL), row(OFF_SX // D_MODEL)),
            pl.BlockSpec((HALO_ROWS, D_MODEL), halo(OFF_SC // D_MODEL)),
            pl.BlockSpec((HALO_ROWS, D_MODEL), halo(OFF_SX // D_MODEL)),
            pl.BlockSpec((tm, D_MODEL), row(OFF_MG // D_MODEL)),
            pl.BlockSpec((tm, D_MODEL), row(OFF_MG // D_MODEL + 1)),
            pl.BlockSpec((tm, D_MODEL), row(OFF_MG // D_MODEL + 2)),
            pl.BlockSpec((tm, LANES), row(OFF_NG // LANES)),
            pl.BlockSpec((tm, D_MODEL), row(0)),
            pl.BlockSpec((tm, D_MODEL), row(0)),
            pl.BlockSpec((tm, D_MODEL), row(0)),
            pl.BlockSpec((CONV_WIDTH, D_MODEL), const),
            pl.BlockSpec((LANES, 3 * D_MODEL), const),
            wspec, wspec, wspec, wspec,
        ],
        out_specs=pl.BlockSpec((tm, D_MODEL), row(0)),
        out_shape=jax.ShapeDtypeStruct((t, D_MODEL), F32),
        compiler_params=_cparams(("arbitrary",)),
        name="merge",
    )(x2, y_ret, z, z, z, z, z, z, z, z, z, o_cmp, o_slc, o_win, conv_w, gate_expand,
      w_ret, w_sc, w_nsa, w_mix)


def _ffn_kernel(x_ref, nw_ref, wa_ref, wv_ref, cwa_ref, cwv_ref, wd_ref, fw_ref, out_ref,
                h_scr, acc_scr, ca_scr, cv_scr, *, tiles_per_seq, final_norm):
    i = pl.program_id(0)
    j = pl.program_id(1)
    first = (i % tiles_per_seq) == 0

    @pl.when(j == 0)
    def _():
        h_scr[...] = _rms(x_ref[...], nw_ref[...]).astype(BF16)
        acc_scr[...] = jnp.zeros_like(acc_scr)

    @pl.when(first)
    def _():
        ca_scr[j] = jnp.zeros(ca_scr.shape[1:], F32)
        cv_scr[j] = jnp.zeros(cv_scr.shape[1:], F32)

    tm = h_scr.shape[0]
    rows_per = tm // FFN_ROW_GROUPS
    prev_a = ca_scr[j]
    prev_v = cv_scr[j]
    for r in range(FFN_ROW_GROUPS):
        rows = slice(r * rows_per, (r + 1) * rows_per)
        h = h_scr[rows, :]
        ua = _dot(h, wa_ref[...])
        uv = _dot(h, wv_ref[...])
        a = _causal_conv3(ua, prev_a, cwa_ref[...])
        v = _causal_conv3(uv, prev_v, cwv_ref[...])
        prev_a = ua[rows_per - HALO_ROWS:, :]
        prev_v = uv[rows_per - HALO_ROWS:, :]
        act = (a * _sigmoid(a) * v).astype(BF16)
        acc_scr[rows, :] += _dot(act, wd_ref[...])
    ca_scr[j] = prev_a
    cv_scr[j] = prev_v

    @pl.when(j == pl.num_programs(1) - 1)
    def _():
        y = x_ref[...] + acc_scr[...]
        if final_norm:
            y = _rms(y, fw_ref[...])
        out_ref[...] = y


def _ffn(x2, norm_w, w_up, conv_w, w_down, final_w, s, tm, tf, final_norm):
    t = x2.shape[0]
    nf = D_FF // tf
    return pl.pallas_call(
        functools.partial(_ffn_kernel, tiles_per_seq=s // tm, final_norm=final_norm),
        grid=(t // tm, nf),
        in_specs=[
            pl.BlockSpec((tm, D_MODEL), lambda i, j: (i, 0)),
            pl.BlockSpec((1, D_MODEL), lambda i, j: (0, 0)),
            pl.BlockSpec((D_MODEL, tf), lambda i, j: (0, j)),
            pl.BlockSpec((D_MODEL, tf), lambda i, j: (0, nf + j)),
            pl.BlockSpec((CONV_WIDTH, tf), lambda i, j: (0, j)),
            pl.BlockSpec((CONV_WIDTH, tf), lambda i, j: (0, nf + j)),
            pl.BlockSpec((tf, D_MODEL), lambda i, j: (j, 0)),
            pl.BlockSpec((1, D_MODEL), lambda i, j: (0, 0)),
        ],
        out_specs=pl.BlockSpec((tm, D_MODEL), lambda i, j: (i, 0)),
        out_shape=jax.ShapeDtypeStruct((t, D_MODEL), F32),
        scratch_shapes=[
            pltpu.VMEM((tm, D_MODEL), BF16),
            pltpu.VMEM((tm, D_MODEL), F32),
            pltpu.VMEM((nf, HALO_ROWS, tf), F32),
            pltpu.VMEM((nf, HALO_ROWS, tf), F32),
        ],
        compiler_params=_cparams(("arbitrary", "arbitrary")),
        name="ffn",
    )(x2, norm_w, w_up, w_up, conv_w, conv_w, w_down, final_w)


def _prep_w_in(w_in_l):
    gw = NSA_KV_HEADS * NSA_DH
    src = OFF_MG
    k_c, v_c, k_s, v_s, k_w, v_w = [w_in_l[:, src + n * gw: src + (n + 1) * gw] for n in range(6)]
    ng = w_in_l[:, src + 6 * gw: src + 6 * gw + 3 * NSA_HEADS]
    ng = ng.reshape(D_MODEL, NSA_HEADS, 3).transpose(0, 2, 1).reshape(D_MODEL, 3 * NSA_HEADS)
    mg = w_in_l[:, src + 6 * gw + 3 * NSA_HEADS:]

    def lane_pad(v):
        v = v.reshape(D_MODEL, NSA_KV_HEADS, NSA_DH)
        return jnp.pad(v, ((0, 0), (0, 0), (0, LANES - NSA_DH))).reshape(D_MODEL, NSA_KV_HEADS * LANES)

    pad = jnp.zeros((D_MODEL, Z_WIDTH - OFF_NG - 3 * NSA_HEADS), w_in_l.dtype)
    return jnp.concatenate([w_in_l[:, :OFF_MG], mg, k_c, v_c, k_s, k_w, lane_pad(v_s), lane_pad(v_w), ng, pad],
                           axis=1).astype(BF16)


def _z_bias():
    bias = np.zeros((1, Z_WIDTH), np.float32)
    for off in (OFF_VS, OFF_VW):
        for g in range(NSA_KV_HEADS):
            bias[0, off + g * LANES + NSA_DH] = 1.0
    return jnp.asarray(bias)


def _gate_expander():
    e = np.zeros((LANES, 3 * D_MODEL), np.float32)
    for br in range(3):
        for h in range(NSA_HEADS):
            e[br * NSA_HEADS + h, br * D_MODEL + h * NSA_DH: br * D_MODEL + (h + 1) * NSA_DH] = 1.0
    return jnp.asarray(e, BF16)


def _overlap_t(s, ncp):
    nc = (s - CMP_BLOCK) // CMP_STRIDE + 1
    ns = s // SLC_BLOCK
    ci = np.arange(ncp) * CMP_STRIDE
    sj = np.arange(ns) * SLC_BLOCK
    ov = ((ci[None, :] < sj[:, None] + SLC_BLOCK) & (ci[None, :] + CMP_BLOCK > sj[:, None])
          & (np.arange(ncp)[None, :] < nc))
    return jnp.asarray(ov.astype(np.float32), BF16)


def _group_major(a, b, s):
    return a.reshape(b, s, NSA_KV_HEADS, NSA_DH).transpose(0, 2, 1, 3).reshape(b * NSA_KV_HEADS, s, NSA_DH)


def kernel(x, attn_norm_w, w_in, ret_norm_w, w_ret_out, sc_conv_w, w_sc_out, nsa_cmp_pos, nsa_cmp_w1,
           nsa_cmp_w2, w_nsa_out, w_mix_out, ffn_norm_w, w_ffn_up, ffn_conv_w, w_ffn_down, final_norm_w):
    b, s, _ = x.shape
    t = b * s
    depth = w_in.shape[0]
    bg = b * NSA_KV_HEADS
    ncp = s // CMP_STRIDE
    assert s % 256 == 0 and s // SLC_BLOCK <= NSA_DH

    tm_in = min(1024, t)
    tm_merge = 256
    tm_ffn = min(1024, s)
    tq_cmp = 512
    tq_sel = 512
    tk_sel = 512
    td_sel = 256
    tq_win = 256
    tr = 256

    ret_consts = _retention_consts(s)
    gate_expand = _gate_expander()
    z_bias = _z_bias()
    ovt = _overlap_t(s, ncp)
    onehot = np.zeros((s, LANES), np.float32)
    onehot[np.arange(s), (np.arange(s) // SLC_BLOCK) % NSA_DH] = 1.0
    blk_onehot = jnp.asarray(onehot, BF16)

    x2 = x.reshape(t, D_MODEL)
    for l in range(depth):
        z = _inproj(x2, attn_norm_w[l][None], _prep_w_in(w_in[l]), z_bias, tm_in, Z_TILE)
        y_ret = _retention(z, ret_norm_w[l][None], ret_consts, b, s, tr)

        def blocks(off):
            r = _group_major(z[:, off:off + NSA_KV_HEADS * NSA_DH], b, s).reshape(bg, ncp, CMP_STRIDE * NSA_DH)
            nxt = jnp.concatenate([r[:, 1:], jnp.zeros_like(r[:, :1])], axis=1)
            return jnp.concatenate([r, nxt], axis=-1)

        blk = jnp.stack([blocks(OFF_KC), blocks(OFF_VC)])
        pos_flat = nsa_cmp_pos[l].reshape(2, 1, CMP_BLOCK * NSA_DH)
        w2p = jnp.pad(nsa_cmp_w2[l], ((0, 0), (0, 0), (0, LANES - NSA_DH))).astype(BF16)
        kvc = _compress(blk, pos_flat, nsa_cmp_w1[l].astype(BF16), w2p)

        o_cmp, sel_bias = _cmp_select(z, kvc, ovt, b, s, tq_cmp)
        o_slc = _flash(z, sel_bias, blk_onehot, b, s, tq_sel, tk_sel, td_sel, "sel")
        o_win = _flash(z, None, None, b, s, tq_win, None, None, "win")

        x2 = _merge(x2, z, y_ret, o_cmp, o_slc, o_win, sc_conv_w[l], gate_expand,
                    w_ret_out[l].astype(BF16), w_sc_out[l].astype(BF16), w_nsa_out[l].astype(BF16),
                    w_mix_out[l].astype(BF16), s, tm_merge)
        x2 = _ffn(x2, ffn_norm_w[l][None], w_ffn_up[l].astype(BF16), ffn_conv_w[l],
                  w_ffn_down[l].astype(BF16), final_norm_w[None], s, tm_ffn, 256, l == depth - 1)
    return x2.reshape(b, s, D_MODEL)
```

```python
import functools

import numpy as np
import jax
import jax.numpy as jnp
from jax import lax
from jax.experimental import pallas as pl
from jax.experimental.pallas import tpu as pltpu

F32 = jnp.float32
BF16 = jnp.bfloat16

D_MODEL = 1024
RET_HEADS = 4
RET_DV = D_MODEL // RET_HEADS
RET_DK = RET_DV // 2
RET_CHUNK = 128
CONV_WIDTH = 3
NSA_DH = 64
NSA_HEADS = D_MODEL // NSA_DH
NSA_KV_HEADS = 4
NSA_HPG = NSA_HEADS // NSA_KV_HEADS
CMP_BLOCK = 32
CMP_STRIDE = 16
CMP_HIDDEN = 256
SLC_BLOCK = 64
N_SELECT = 16
WINDOW = 512
D_FF = ((8 * D_MODEL // 3 + 127) // 128) * 128
EPS = 1e-6
NEG = -1e30
FORCE = 1e6

WIN_STACK = 2
LANES = 128
HALO_ROWS = 8

OFF_RQ = 0
OFF_RK = 512
OFF_RV = 1024
OFF_RG = 2048
OFF_SB = 3072
OFF_SC = 4096
OFF_SX = 5120
OFF_NQ = 6144
OFF_MG = 7168
OFF_KC = 10240
OFF_VC = 10496
OFF_KS = 10752
OFF_KW = 11008
OFF_VS = 11264
OFF_VW = 11776
OFF_NG = 12288
Z_WIDTH = 12800
Z_TILE = 1280

VMEM_LIMIT = 56 * 1024 * 1024


def _cparams(sem):
    return pltpu.CompilerParams(dimension_semantics=sem, vmem_limit_bytes=VMEM_LIMIT)


def _nt_dot(a, b):
    return lax.dot_general(a, b, (((1,), (1,)), ((), ())), preferred_element_type=F32)


def _dot(a, b):
    return jnp.dot(a, b, preferred_element_type=F32)


def _sigmoid(x):
    return 1.0 / (1.0 + jnp.exp(-x))


def _rms(x, w):
    return x * lax.rsqrt(jnp.mean(x * x, axis=-1, keepdims=True) + EPS) * w


def _inproj_kernel(x_ref, nw_ref, w_ref, b_ref, z_ref, h_scr):
    @pl.when(pl.program_id(1) == 0)
    def _():
        h_scr[...] = _rms(x_ref[...], nw_ref[...]).astype(BF16)

    z_ref[...] = (_dot(h_scr[...], w_ref[...]) + b_ref[...]).astype(BF16)


def _inproj(x2, norm_w, w_in_p, z_bias, tm, tn):
    t = x2.shape[0]
    return pl.pallas_call(
        _inproj_kernel,
        grid=(t // tm, Z_WIDTH // tn),
        in_specs=[
            pl.BlockSpec((tm, D_MODEL), lambda i, j: (i, 0)),
            pl.BlockSpec((1, D_MODEL), lambda i, j: (0, 0)),
            pl.BlockSpec((D_MODEL, tn), lambda i, j: (0, j)),
            pl.BlockSpec((1, tn), lambda i, j: (0, j)),
        ],
        out_specs=pl.BlockSpec((tm, tn), lambda i, j: (i, j)),
        out_shape=jax.ShapeDtypeStruct((t, Z_WIDTH), BF16),
        scratch_shapes=[pltpu.VMEM((tm, D_MODEL), BF16)],
        compiler_params=_cparams(("arbitrary", "arbitrary")),
        name="inproj",
    )(x2, norm_w, w_in_p, z_bias)


def _ret_kernel(q_ref, k_ref, v_ref, g_ref, cos_ref, sin_ref, dm_ref, zeta_ref, xi_ref,
                dec_ref, nw_ref, y_ref, st_scr, *, n_chunks):
    @pl.when(pl.program_id(1) == 0)
    def _():
        st_scr[...] = jnp.zeros_like(st_scr)

    c = RET_CHUNK
    for ci in range(n_chunks):
        rows = slice(ci * c, (ci + 1) * c)
        cos = cos_ref[rows, :]
        sin = sin_ref[rows, :]
        for h in range(RET_HEADS):
            q = q_ref[rows, h * RET_DK:(h + 1) * RET_DK].astype(F32)
            k = k_ref[rows, h * RET_DK:(h + 1) * RET_DK].astype(F32)
            qr = q * cos + pltpu.roll(q, RET_DK // 2, axis=1) * sin
            kr = (k * cos + pltpu.roll(k, RET_DK // 2, axis=1) * sin) * (RET_DK ** -0.5)
            qb = qr.astype(BF16)
            kb = kr.astype(BF16)
            v = v_ref[rows, h * RET_DV:(h + 1) * RET_DV]
            scores = _nt_dot(qb, kb) * dm_ref[h]
            o = _dot(scores.astype(BF16), v)
            st = st_scr[h]
            o = o + _dot(qb, st.astype(BF16)) * xi_ref[h]
            vz = (v.astype(F32) * zeta_ref[h]).astype(BF16)
            kv = _dot(kr.T.astype(BF16), vz)
            st_scr[h] = st * dec_ref[h] + kv
            mu = jnp.mean(o, axis=-1, keepdims=True)
            d = o - mu
            var = jnp.mean(d * d, axis=-1, keepdims=True)
            on = d * lax.rsqrt(var + EPS) * nw_ref[:, h * RET_DV:(h + 1) * RET_DV]
            g = g_ref[rows, h * RET_DV:(h + 1) * RET_DV].astype(F32)
            y_ref[rows, h * RET_DV:(h + 1) * RET_DV] = (g * _sigmoid(g) * on).astype(BF16)


def _retention_consts(s):
    c = RET_CHUNK
    pos = jnp.arange(s, dtype=F32)
    theta = 10000.0 ** (-jnp.linspace(0.0, 1.0, RET_DK // 2, dtype=F32))
    ang = pos[:, None] * theta[None, :]
    cos = jnp.cos(ang)
    sin = jnp.sin(ang)
    cos2 = jnp.concatenate([cos, cos], axis=-1)
    sin2 = jnp.concatenate([-sin, sin], axis=-1)
    log_gamma = jnp.log1p(-(2.0 ** (-5.0 - jnp.arange(RET_HEADS, dtype=F32))))
    j = jnp.arange(c, dtype=F32)
    rel = j[:, None] - j[None, :]
    dmask = jnp.where(rel >= 0, jnp.exp(log_gamma[:, None, None] * jnp.maximum(rel, 0.0)), 0.0)
    zeta = jnp.exp(log_gamma[:, None] * (c - 1 - j)[None, :])
    xi = jnp.exp(log_gamma[:, None] * (j + 1.0)[None, :])
    dec = jnp.exp(log_gamma * c)
    zeta_b = jnp.broadcast_to(zeta[:, :, None], (RET_HEADS, c, RET_DV))
    xi_b = jnp.broadcast_to(xi[:, :, None], (RET_HEADS, c, RET_DV))
    dec_b = jnp.broadcast_to(dec[:, None, None], (RET_HEADS, RET_DK, RET_DV))
    return cos2, sin2, dmask, zeta_b, xi_b, dec_b


def _retention(z, ret_norm_w, consts, b, s, tr):
    t = b * s
    nt = s // tr
    cos2, sin2, dmask, zeta_b, xi_b, dec_b = consts
    const3 = lambda bi, i: (0, 0, 0)
    return pl.pallas_call(
        functools.partial(_ret_kernel, n_chunks=tr // RET_CHUNK),
        grid=(b, nt),
        in_specs=[
            pl.BlockSpec((tr, 512), lambda bi, i: (bi * nt + i, OFF_RQ // 512)),
            pl.BlockSpec((tr, 512), lambda bi, i: (bi * nt + i, OFF_RK // 512)),
            pl.BlockSpec((tr, 1024), lambda bi, i: (bi * nt + i, OFF_RV // 1024)),
            pl.BlockSpec((tr, 1024), lambda bi, i: (bi * nt + i, OFF_RG // 1024)),
            pl.BlockSpec((tr, RET_DK), lambda bi, i: (i, 0)),
            pl.BlockSpec((tr, RET_DK), lambda bi, i: (i, 0)),
            pl.BlockSpec((RET_HEADS, RET_CHUNK, RET_CHUNK), const3),
            pl.BlockSpec((RET_HEADS, RET_CHUNK, RET_DV), const3),
            pl.BlockSpec((RET_HEADS, RET_CHUNK, RET_DV), const3),
            pl.BlockSpec((RET_HEADS, RET_DK, RET_DV), const3),
            pl.BlockSpec((1, D_MODEL), lambda bi, i: (0, 0)),
        ],
        out_specs=pl.BlockSpec((tr, D_MODEL), lambda bi, i: (bi * nt + i, 0)),
        out_shape=jax.ShapeDtypeStruct((t, D_MODEL), BF16),
        scratch_shapes=[pltpu.VMEM((RET_HEADS, RET_DK, RET_DV), F32)],
        compiler_params=_cparams(("arbitrary", "arbitrary")),
        name="retention",
    )(z, z, z, z, cos2, sin2, dmask, zeta_b, xi_b, dec_b, ret_norm_w)


def _compress_kernel(rows_ref, pos_ref, w1_ref, w2_ref, out_ref):
    half = rows_ref.shape[-1]
    x = rows_ref[0, 0].astype(F32)
    lo = _dot((x + pos_ref[0, :, :half]).astype(BF16), w1_ref[0, :half, :])
    hi = _dot((x + pos_ref[0, :, half:]).astype(BF16), w1_ref[0, half:, :])
    hid = lo + pltpu.roll(hi, hi.shape[0] - 1, axis=0)
    gel = 0.5 * hid * (1.0 + jnp.tanh(np.sqrt(2.0 / np.pi) * (hid + 0.044715 * (hid * hid * hid))))
    out_ref[0, 0] = _dot(gel.astype(BF16), w2_ref[0]).astype(BF16)


def _compress(rows, pos_flat, w1, w2p):
    _, bg, ncp, half = rows.shape
    width = 2 * half
    return pl.pallas_call(
        _compress_kernel,
        grid=(2, bg),
        in_specs=[
            pl.BlockSpec((1, 1, ncp, half), lambda a, n: (a, n, 0, 0)),
            pl.BlockSpec((1, 1, width), lambda a, n: (a, 0, 0)),
            pl.BlockSpec((1, width, CMP_HIDDEN), lambda a, n: (a, 0, 0)),
            pl.BlockSpec((1, CMP_HIDDEN, LANES), lambda a, n: (a, 0, 0)),
        ],
        out_specs=pl.BlockSpec((1, 1, ncp, LANES), lambda a, n: (a, n, 0, 0)),
        out_shape=jax.ShapeDtypeStruct((2, bg, ncp, LANES), BF16),
        compiler_params=_cparams(("arbitrary", "arbitrary")),
        name="nsa_compress",
    )(rows, pos_flat, w1, w2p)


def _head_pad(q2_ref_block, h):
    x = q2_ref_block[:, (h // 2) * LANES:(h // 2 + 1) * LANES].astype(F32)
    if h % 2:
        x = pltpu.roll(x, NSA_DH, axis=1)
    lane = lax.broadcasted_iota(jnp.int32, x.shape, 1)
    return jnp.where(lane < NSA_DH, x, 0.0)


def _merge_heads(parts):
    lo = parts[0] + pltpu.roll(parts[1], NSA_DH, axis=1)
    hi = parts[2] + pltpu.roll(parts[3], NSA_DH, axis=1)
    return jnp.concatenate([lo, hi], axis=1)


def _cmp_select_kernel(q_ref, kc_ref, vc_ref, ovt_ref, ocmp_ref, bias_ref, *, tq, n_blk):
    t0 = pl.program_id(1) * tq
    ncp = kc_ref.shape[2]
    kc = kc_ref[0, 0]
    vc = vc_ref[0, 0]
    ovt = ovt_ref[...]
    scale = NSA_DH ** -0.5

    t_q = t0 + lax.broadcasted_iota(jnp.int32, (tq, ncp), 0)
    c_id = lax.broadcasted_iota(jnp.int32, (tq, ncp), 1)
    valid = (c_id * CMP_STRIDE + (CMP_BLOCK - 1)) <= t_q
    t_col = t0 + lax.broadcasted_iota(jnp.int32, (tq, 1), 0)
    any1 = jnp.where(t_col >= CMP_BLOCK - 1, 1.0, 0.0)

    imp = jnp.zeros((n_blk, tq), F32)
    o_parts = []
    for h in range(NSA_HPG):
        qp = (_head_pad(q_ref, h) * scale).astype(BF16)
        s1 = jnp.where(valid, _nt_dot(qp, kc), NEG)
        m = jnp.max(s1, axis=-1, keepdims=True)
        e = jnp.exp(s1 - m)
        pb = (e * ((1.0 / jnp.sum(e, axis=-1, keepdims=True)) * any1)).astype(BF16)
        o_parts.append(_dot(pb, vc))
        imp = imp + _nt_dot(ovt, pb)
    ocmp_ref[...] = _merge_heads(o_parts).astype(BF16)

    blk = lax.broadcasted_iota(jnp.int32, (n_blk, tq), 0)
    t_row = t0 + lax.broadcasted_iota(jnp.int32, (n_blk, tq), 1)
    cur = t_row // SLC_BLOCK
    causal = blk <= cur
    forced = (blk == 0) | (blk == cur) | (blk == cur - 1)
    imp = jnp.where(forced, FORCE, imp)
    imp = jnp.where(causal, imp, NEG)
    sub = 8
    groups = [imp[g * sub:(g + 1) * sub, :] for g in range(n_blk // sub)]
    ranks = [jnp.zeros((sub, tq), F32) for _ in groups]
    row_in_group = lax.broadcasted_iota(jnp.int32, (sub, tq), 0)
    for i in range(n_blk):
        r = jnp.broadcast_to(imp[i:i + 1, :], (sub, tq))
        for g, x in enumerate(groups):
            if g < i // sub:
                ahead = jnp.where(r > x, 1.0, 0.0)
            elif g > i // sub:
                ahead = jnp.where(r >= x, 1.0, 0.0)
            else:
                ahead = jnp.where(row_in_group > i % sub, jnp.where(r >= x, 1.0, 0.0), jnp.where(r > x, 1.0, 0.0))
            ranks[g] = ranks[g] + ahead
    rank = jnp.concatenate(ranks, axis=0)
    bias_t = jnp.where(rank < float(min(N_SELECT, n_blk)), 0.0, NEG)
    pieces = [bias_t]
    if n_blk < NSA_DH:
        pieces.append(jnp.full((NSA_DH - n_blk, tq), NEG, F32))
    pieces.append(jnp.zeros((LANES - NSA_DH, tq), F32))
    bias_ref[0] = jnp.concatenate(pieces, axis=0).T.astype(BF16)


def _cmp_select(z, kvc, ovt, b, s, tq):
    t = b * s
    nq = s // tq
    bg = b * NSA_KV_HEADS
    ncp = kvc.shape[2]
    n_blk = s // SLC_BLOCK
    qcol = OFF_NQ // 256
    return pl.pallas_call(
        functools.partial(_cmp_select_kernel, tq=tq, n_blk=n_blk),
        grid=(bg, nq),
        in_specs=[
            pl.BlockSpec((tq, 256), lambda n, i: ((n // NSA_KV_HEADS) * nq + i, qcol + n % NSA_KV_HEADS)),
            pl.BlockSpec((1, 1, ncp, LANES), lambda n, i: (0, n, 0, 0)),
            pl.BlockSpec((1, 1, ncp, LANES), lambda n, i: (1, n, 0, 0)),
            pl.BlockSpec((n_blk, ncp), lambda n, i: (0, 0)),
        ],
        out_specs=[
            pl.BlockSpec((tq, 256), lambda n, i: ((n // NSA_KV_HEADS) * nq + i, n % NSA_KV_HEADS)),
            pl.BlockSpec((1, tq, LANES), lambda n, i: (n, i, 0)),
        ],
        out_shape=[
            jax.ShapeDtypeStruct((t, D_MODEL), BF16),
            jax.ShapeDtypeStruct((bg, s, LANES), BF16),
        ],
        compiler_params=_cparams(("arbitrary", "arbitrary")),
        name="nsa_cmp_select",
    )(z, kvc, kvc, ovt)


def _normalised_heads(acc_of_head):
    outs = []
    for h in range(NSA_HPG):
        acc = acc_of_head(h)
        lane = lax.broadcasted_iota(jnp.int32, acc.shape, 1)
        denom = jnp.sum(jnp.where(lane == NSA_DH, acc, 0.0), axis=-1, keepdims=True)
        outs.append(jnp.where(lane < NSA_DH, acc * (1.0 / denom), 0.0))
    return _merge_heads(outs).astype(BF16)


def _q_pair(q_ref, h, odd_group):
    x = q_ref[:, (h // 2) * LANES:(h // 2 + 1) * LANES].astype(F32) * (NSA_DH ** -0.5)
    odd_head = jnp.bool_(h % 2 == 1)
    x = jnp.where(odd_head != odd_group, pltpu.roll(x, NSA_DH, axis=1), x)
    lane_half = lax.broadcasted_iota(jnp.int32, x.shape, 1) // NSA_DH
    return jnp.where(lane_half == odd_group.astype(jnp.int32), x, 0.0).astype(BF16)


def _flash_sel_kernel(q_ref, b_ref, k_ref, oh_ref, v_ref, o_ref, qa_scr, m_scr, acc_scr, *, tq, tk, td):
    t0 = pl.program_id(1) * tq
    odd_group = (pl.program_id(0) % 2) == 1
    for h in range(NSA_HPG):
        qa_scr[h] = jnp.concatenate([_q_pair(q_ref, h, odd_group), b_ref[0]], axis=1)
    m_scr[...] = jnp.full(m_scr.shape, NEG, F32)
    acc_scr[...] = jnp.zeros(acc_scr.shape, F32)

    def tile(start, row0, width, masked):
        k = jnp.concatenate([k_ref[pl.ds(start, width), :], oh_ref[pl.ds(start, width), :]], axis=1)
        v = v_ref[pl.ds(start, width), :]
        nr = tq - row0
        if masked:
            t_q = t0 + row0 + lax.broadcasted_iota(jnp.int32, (nr, width), 0)
            ok = (start + lax.broadcasted_iota(jnp.int32, (nr, width), 1)) <= t_q
        for h in range(NSA_HPG):
            sc = _nt_dot(qa_scr[h, row0:tq, :], k)
            if masked:
                sc = jnp.where(ok, sc, NEG)
            m_prev = m_scr[h, row0:tq, :]
            m_new = jnp.maximum(m_prev, jnp.max(sc, axis=-1, keepdims=True))
            alpha = jnp.exp(m_prev - m_new)
            p = jnp.exp(sc - jnp.concatenate([m_new] * (width // LANES), axis=1))
            acc_scr[h, row0:tq, :] = alpha * acc_scr[h, row0:tq, :] + _dot(p.astype(BF16), v)
            m_scr[h, row0:tq, :] = m_new

    def body(kt, carry):
        tile(pl.multiple_of(kt * tk, tk), 0, tk, False)
        return carry

    lax.fori_loop(0, t0 // tk, body, 0)
    for j in range(tq // td):
        tile(pl.multiple_of(t0 + j * td, td), j * td, td, True)
    o_ref[...] = _normalised_heads(lambda h: acc_scr[h])


def _flash_win_kernel(q_ref, k_ref, v_ref, band_ref, o_ref, *, tq):
    t0 = pl.program_id(1) * tq
    odd_group = (pl.program_id(0) % 2) == 1
    n_tiles = WINDOW // tq + 1
    ks, vs, biases = [], [], []
    for j in range(n_tiles):
        first = t0 + (j - n_tiles + 1) * tq
        start = pl.multiple_of(jnp.maximum(first, 0), tq)
        ks.append(k_ref[pl.ds(start, tq), :])
        vs.append(v_ref[pl.ds(start, tq), :])
        biases.append(jnp.where(first < 0, NEG, band_ref[j]))

    accs = []
    for h0 in range(0, NSA_HPG, WIN_STACK):
        q = jnp.concatenate([_q_pair(q_ref, h0 + d, odd_group) for d in range(WIN_STACK)], axis=0)
        scs = [_nt_dot(q, ks[j]) + jnp.concatenate([biases[j]] * WIN_STACK, axis=0) for j in range(n_tiles)]
        m = jnp.max(scs[0], axis=-1, keepdims=True)
        for sc in scs[1:]:
            m = jnp.maximum(m, jnp.max(sc, axis=-1, keepdims=True))
        acc = _dot(jnp.exp(scs[0] - m).astype(BF16), vs[0])
        for j in range(1, n_tiles):
            acc = acc + _dot(jnp.exp(scs[j] - m).astype(BF16), vs[j])
        accs.extend(acc[d * tq:(d + 1) * tq] for d in range(WIN_STACK))

    o_ref[...] = _normalised_heads(lambda h: accs[h])


def _flash(z, sel_bias, blk_onehot, b, s, tq, tk, td, mode):
    t = b * s
    nq = s // tq
    bg = b * NSA_KV_HEADS
    grp = lambda n: n % NSA_KV_HEADS
    bat = lambda n: n // NSA_KV_HEADS
    off_k, off_v = (OFF_KS, OFF_VS) if mode == "sel" else (OFF_KW, OFF_VW)
    q_spec = pl.BlockSpec((tq, 256), lambda n, i: (bat(n) * nq + i, OFF_NQ // 256 + grp(n)))
    k_spec = pl.BlockSpec((s, LANES), lambda n, i: (bat(n), off_k // LANES + grp(n) // 2))
    v_spec = pl.BlockSpec((s, LANES), lambda n, i: (bat(n), off_v // LANES + grp(n)))
    if mode == "sel":
        body = functools.partial(_flash_sel_kernel, tq=tq, tk=tk, td=td)
        in_specs = [q_spec, pl.BlockSpec((1, tq, LANES), lambda n, i: (n, i, 0)), k_spec,
                    pl.BlockSpec((s, LANES), lambda n, i: (0, 0)), v_spec]
        args = (z, sel_bias, z, blk_onehot, z)
        scratch = [pltpu.VMEM((NSA_HPG, tq, 2 * LANES), BF16), pltpu.VMEM((NSA_HPG, tq, LANES), F32),
                   pltpu.VMEM((NSA_HPG, tq, LANES), F32)]
    else:
        body = functools.partial(_flash_win_kernel, tq=tq)
        n_tiles = WINDOW // tq + 1
        rel = (np.arange(tq)[None, :, None] - np.arange(tq)[None, None, :]
               + (n_tiles - 1 - np.arange(n_tiles))[:, None, None] * tq)
        band = jnp.asarray(np.where((rel >= 0) & (rel < WINDOW), 0.0, NEG), F32)
        in_specs = [q_spec, k_spec, v_spec, pl.BlockSpec((n_tiles, tq, tq), lambda n, i: (0, 0, 0))]
        args = (z, z, z, band)
        scratch = []
    return pl.pallas_call(
        body,
        grid=(bg, nq),
        in_specs=in_specs,
        out_specs=pl.BlockSpec((tq, 256), lambda n, i: (bat(n) * nq + i, grp(n))),
        out_shape=jax.ShapeDtypeStruct((t, D_MODEL), BF16),
        scratch_shapes=scratch,
        compiler_params=_cparams(("arbitrary", "arbitrary")),
        name="nsa_flash_" + mode,
    )(*args)


def _shifted(u, prev, n):
    rolled = pltpu.roll(u, n, axis=0)
    head = rolled[:HALO_ROWS]
    row = lax.broadcasted_iota(jnp.int32, head.shape, 0)
    for r in range(n):
        head = jnp.where(row == r, prev[HALO_ROWS - n + r:HALO_ROWS - n + r + 1, :], head)
    return jnp.concatenate([head, rolled[HALO_ROWS:]], axis=0)


def _causal_conv3(u, prev, w):
    return w[0:1, :] * _shifted(u, prev, 2) + w[1:2, :] * _shifted(u, prev, 1) + w[2:3, :] * u


def _merge_kernel(x_ref, yret_ref, sb_ref, sc_ref, sx_ref, sch_ref, sxh_ref, mg0_ref, mg1_ref, mg2_ref,
                  ng_ref, ocmp_ref, oslc_ref, owin_ref, cw_ref, e_ref, wret_ref, wsc_ref, wnsa_ref,
                  wmix_ref, out_ref, *, tiles_per_seq):
    first = (pl.program_id(0) % tiles_per_seq) == 0
    u = sc_ref[...].astype(F32) * sx_ref[...].astype(F32)
    prev = sch_ref[...].astype(F32) * sxh_ref[...].astype(F32)
    prev = jnp.where(first, 0.0, prev)
    y_sc = (sb_ref[...].astype(F32) * _causal_conv3(u, prev, cw_ref[...])).astype(BF16)

    gates = _dot(_sigmoid(ng_ref[...].astype(F32)).astype(BF16), e_ref[...])
    y_nsa = (gates[:, 0:D_MODEL] * ocmp_ref[...].astype(F32)
             + gates[:, D_MODEL:2 * D_MODEL] * oslc_ref[...].astype(F32)
             + gates[:, 2 * D_MODEL:3 * D_MODEL] * owin_ref[...].astype(F32)).astype(BF16)

    merged = (_sigmoid(mg0_ref[...].astype(F32)) * _dot(yret_ref[...], wret_ref[...])
              + _sigmoid(mg1_ref[...].astype(F32)) * _dot(y_sc, wsc_ref[...])
              + _sigmoid(mg2_ref[...].astype(F32)) * _dot(y_nsa, wnsa_ref[...]))
    out_ref[...] = x_ref[...] + _dot(merged.astype(BF16), wmix_ref[...])


def _merge(x2, z, y_ret, o_cmp, o_slc, o_win, conv_w, gate_expand, w_ret, w_sc, w_nsa, w_mix, s, tm):
    t = x2.shape[0]
    row = lambda c: (lambda i: (i, c))
    halo = lambda c: (lambda i: (jnp.maximum(i * (tm // HALO_ROWS) - 1, 0), c))
    const = lambda i: (0, 0)
    wspec = pl.BlockSpec((D_MODEL, D_MODEL), const)
    return pl.pallas_call(
        functools.partial(_merge_kernel, tiles_per_seq=s // tm),
        grid=(t // tm,),
        in_specs=[
            pl.BlockSpec((tm, D_MODEL), row(0)),
            pl.BlockSpec((tm, D_MODEL), row(0)),
            pl.BlockSpec((tm, D_MODEL), row(OFF_SB // D_MODEL)),
            pl.BlockSpec((tm, D_MODEL), row(OFF_SC // D_MODEL)),
            pl.BlockSpec((tm, D_MODEL), row(OFF_SX // D_MODEL)),
            pl.BlockSpec((HALO_ROWS, D_MODEL), halo(OFF_SC // D_MODEL)),
            pl.BlockSpec((HALO_ROWS, D_MODEL), halo(OFF_SX // D_MODEL)),
            pl.BlockSpec((tm, D_MODEL), row(OFF_MG // D_MODEL)),
            pl.BlockSpec((tm, D_MODEL), row(OFF_MG // D_MODEL + 1)),
            pl.BlockSpec((tm, D_MODEL), row(OFF_MG // D_MODEL + 2)),
            pl.BlockSpec((tm, LANES), row(OFF_NG // LANES)),
            pl.BlockSpec((tm, D_MODEL), row(0)),
            pl.BlockSpec((tm, D_MODEL), row(0)),
            pl.BlockSpec((tm, D_MODEL), row(0)),
            pl.BlockSpec((CONV_WIDTH, D_MODEL), const),
            pl.BlockSpec((LANES, 3 * D_MODEL), const),
            wspec, wspec, wspec, wspec,
        ],
        out_specs=pl.BlockSpec((tm, D_MODEL), row(0)),
        out_shape=jax.ShapeDtypeStruct((t, D_MODEL), F32),
        compiler_params=_cparams(("arbitrary",)),
        name="merge",
    )(x2, y_ret, z, z, z, z, z, z, z, z, z, o_cmp, o_slc, o_win, conv_w, gate_expand,
      w_ret, w_sc, w_nsa, w_mix)


def _ffn_kernel(x_ref, nw_ref, wa_ref, wv_ref, cwa_ref, cwv_ref, wd_ref, fw_ref, out_ref,
                h_scr, acc_scr, ca_scr, cv_scr, *, tiles_per_seq, final_norm):
    i = pl.program_id(0)
    j = pl.program_id(1)
    first = (i % tiles_per_seq) == 0

    @pl.when(j == 0)
    def _():
        h_scr[...] = _rms(x_ref[...], nw_ref[...]).astype(BF16)
        acc_scr[...] = jnp.zeros_like(acc_scr)

    @pl.when(first)
    def _():
        ca_scr[j] = jnp.zeros(ca_scr.shape[1:], F32)
        cv_scr[j] = jnp.zeros(cv_scr.shape[1:], F32)

    h = h_scr[...]
    tm = h.shape[0]
    ua = _dot(h, wa_ref[...])
    uv = _dot(h, wv_ref[...])
    a = _causal_conv3(ua, ca_scr[j], cwa_ref[...])
    v = _causal_conv3(uv, cv_scr[j], cwv_ref[...])
    ca_scr[j] = ua[tm - HALO_ROWS:, :]
    cv_scr[j] = uv[tm - HALO_ROWS:, :]
    act = (a * _sigmoid(a) * v).astype(BF16)
    acc_scr[...] += _dot(act, wd_ref[...])

    @pl.when(j == pl.num_programs(1) - 1)
    def _():
        y = x_ref[...] + acc_scr[...]
        if final_norm:
            y = _rms(y, fw_ref[...])
        out_ref[...] = y


def _ffn(x2, norm_w, w_up, conv_w, w_down, final_w, s, tm, tf, final_norm):
    t = x2.shape[0]
    nf = D_FF // tf
    return pl.pallas_call(
        functools.partial(_ffn_kernel, tiles_per_seq=s // tm, final_norm=final_norm),
        grid=(t // tm, nf),
        in_specs=[
            pl.BlockSpec((tm, D_MODEL), lambda i, j: (i, 0)),
            pl.BlockSpec((1, D_MODEL), lambda i, j: (0, 0)),
            pl.BlockSpec((D_MODEL, tf), lambda i, j: (0, j)),
            pl.BlockSpec((D_MODEL, tf), lambda i, j: (0, nf + j)),
            pl.BlockSpec((CONV_WIDTH, tf), lambda i, j: (0, j)),
            pl.BlockSpec((CONV_WIDTH, tf), lambda i, j: (0, nf + j)),
            pl.BlockSpec((tf, D_MODEL), lambda i, j: (j, 0)),
            pl.BlockSpec((1, D_MODEL), lambda i, j: (0, 0)),
        ],
        out_specs=pl.BlockSpec((tm, D_MODEL), lambda i, j: (i, 0)),
        out_shape=jax.ShapeDtypeStruct((t, D_MODEL), F32),
        scratch_shapes=[
            pltpu.VMEM((tm, D_MODEL), BF16),
            pltpu.VMEM((tm, D_MODEL), F32),
            pltpu.VMEM((nf, HALO_ROWS, tf), F32),
            pltpu.VMEM((nf, HALO_ROWS, tf), F32),
        ],
        compiler_params=_cparams(("arbitrary", "arbitrary")),
        name="ffn",
    )(x2, norm_w, w_up, w_up, conv_w, conv_w, w_down, final_w)


def _prep_w_in(w_in_l):
    gw = NSA_KV_HEADS * NSA_DH
    src = OFF_MG
    k_c, v_c, k_s, v_s, k_w, v_w = [w_in_l[:, src + n * gw: src + (n + 1) * gw] for n in range(6)]
    ng = w_in_l[:, src + 6 * gw: src + 6 * gw + 3 * NSA_HEADS]
    ng = ng.reshape(D_MODEL, NSA_HEADS, 3).transpose(0, 2, 1).reshape(D_MODEL, 3 * NSA_HEADS)
    mg = w_in_l[:, src + 6 * gw + 3 * NSA_HEADS:]

    def lane_pad(v):
        v = v.reshape(D_MODEL, NSA_KV_HEADS, NSA_DH)
        return jnp.pad(v, ((0, 0), (0, 0), (0, LANES - NSA_DH))).reshape(D_MODEL, NSA_KV_HEADS * LANES)

    pad = jnp.zeros((D_MODEL, Z_WIDTH - OFF_NG - 3 * NSA_HEADS), w_in_l.dtype)
    return jnp.concatenate([w_in_l[:, :OFF_MG], mg, k_c, v_c, k_s, k_w, lane_pad(v_s), lane_pad(v_w), ng, pad],
                           axis=1).astype(BF16)


def _z_bias():
    bias = np.zeros((1, Z_WIDTH), np.float32)
    for off in (OFF_VS, OFF_VW):
        for g in range(NSA_KV_HEADS):
            bias[0, off + g * LANES + NSA_DH] = 1.0
    return jnp.asarray(bias)


def _gate_expander():
    e = np.zeros((LANES, 3 * D_MODEL), np.float32)
    for br in range(3):
        for h in range(NSA_HEADS):
            e[br * NSA_HEADS + h, br * D_MODEL + h * NSA_DH: br * D_MODEL + (h + 1) * NSA_DH] = 1.0
    return jnp.asarray(e, BF16)


def _overlap_t(s, ncp):
    nc = (s - CMP_BLOCK) // CMP_STRIDE + 1
    ns = s // SLC_BLOCK
    ci = np.arange(ncp) * CMP_STRIDE
    sj = np.arange(ns) * SLC_BLOCK
    ov = ((ci[None, :] < sj[:, None] + SLC_BLOCK) & (ci[None, :] + CMP_BLOCK > sj[:, None])
          & (np.arange(ncp)[None, :] < nc))
    return jnp.asarray(ov.astype(np.float32), BF16)


def kernel(x, attn_norm_w, w_in, ret_norm_w, w_ret_out, sc_conv_w, w_sc_out, nsa_cmp_pos, nsa_cmp_w1,
           nsa_cmp_w2, w_nsa_out, w_mix_out, ffn_norm_w, w_ffn_up, ffn_conv_w, w_ffn_down, final_norm_w):
    b, s, _ = x.shape
    t = b * s
    depth = w_in.shape[0]
    bg = b * NSA_KV_HEADS
    ncp = s // CMP_STRIDE
    assert s % 256 == 0 and s // SLC_BLOCK <= NSA_DH

    tm_in = min(1024, t)
    tm_merge = 256
    tm_ffn = min(1024, s)
    tq_cmp = 512
    tq_sel = 1024
    tk_sel = 512
    td_sel = 512
    tq_win = 256
    tr = 256

    ret_consts = _retention_consts(s)
    gate_expand = _gate_expander()
    z_bias = _z_bias()
    ovt = _overlap_t(s, ncp)
    onehot = np.zeros((s, LANES), np.float32)
    onehot[np.arange(s), (np.arange(s) // SLC_BLOCK) % NSA_DH] = 1.0
    blk_onehot = jnp.asarray(onehot, BF16)

    x2 = x.reshape(t, D_MODEL)
    for l in range(depth):
        z = _inproj(x2, attn_norm_w[l][None], _prep_w_in(w_in[l]), z_bias, tm_in, Z_TILE)
        y_ret = _retention(z, ret_norm_w[l][None], ret_consts, b, s, tr)

        kv_c = z[:, OFF_KC:OFF_KC + 2 * NSA_KV_HEADS * NSA_DH]
        kv_c = kv_c.reshape(b, ncp, CMP_STRIDE, 2, NSA_KV_HEADS, NSA_DH).transpose(3, 0, 4, 1, 2, 5)
        kv_c = kv_c.reshape(2, bg, ncp, CMP_STRIDE * NSA_DH)
        pos_flat = nsa_cmp_pos[l].reshape(2, 1, CMP_BLOCK * NSA_DH)
        w2p = jnp.pad(nsa_cmp_w2[l], ((0, 0), (0, 0), (0, LANES - NSA_DH))).astype(BF16)
        kvc = _compress(kv_c, pos_flat, nsa_cmp_w1[l].astype(BF16), w2p)

        o_cmp, sel_bias = _cmp_select(z, kvc, ovt, b, s, tq_cmp)
        o_slc = _flash(z, sel_bias, blk_onehot, b, s, tq_sel, tk_sel, td_sel, "sel")
        o_win = _flash(z, None, None, b, s, tq_win, None, None, "win")

        x2 = _merge(x2, z, y_ret, o_cmp, o_slc, o_win, sc_conv_w[l], gate_expand,
                    w_ret_out[l].astype(BF16), w_sc_out[l].astype(BF16), w_nsa_out[l].astype(BF16),
                    w_mix_out[l].astype(BF16), s, tm_merge)
        x2 = _ffn(x2, ffn_norm_w[l][None], w_ffn_up[l].astype(BF16), ffn_conv_w[l],
                  w_ffn_down[l].astype(BF16), final_norm_w[None], s, tm_ffn, 256, l == depth - 1)
    return x2.reshape(b, s, D_MODEL)
```

```python
import functools

import numpy as np
import jax
import jax.numpy as jnp
from jax import lax
from jax.experimental import pallas as pl
from jax.experimental.pallas import tpu as pltpu

F32 = jnp.float32
BF16 = jnp.bfloat16

D_MODEL = 1024
RET_HEADS = 4
RET_DV = D_MODEL // RET_HEADS
RET_DK = RET_DV // 2
RET_CHUNK = 128
CONV_WIDTH = 3
NSA_DH = 64
NSA_HEADS = D_MODEL // NSA_DH
NSA_KV_HEADS = 4
NSA_HPG = NSA_HEADS // NSA_KV_HEADS
CMP_BLOCK = 32
CMP_STRIDE = 16
CMP_HIDDEN = 256
SLC_BLOCK = 64
N_SELECT = 16
WINDOW = 512
D_FF = ((8 * D_MODEL // 3 + 127) // 128) * 128
EPS = 1e-6
NEG = -1e30
FORCE = 1e6

WIN_STACK = 2
LANES = 128
HALO_ROWS = 8

OFF_RQ = 0
OFF_RK = 512
OFF_RV = 1024
OFF_RG = 2048
OFF_SB = 3072
OFF_SC = 4096
OFF_SX = 5120
OFF_NQ = 6144
OFF_MG = 7168
OFF_KC = 10240
OFF_VC = 10496
OFF_KS = 10752
OFF_KW = 11008
OFF_VS = 11264
OFF_VW = 11776
OFF_NG = 12288
Z_WIDTH = 12800
Z_TILE = 1280

VMEM_LIMIT = 56 * 1024 * 1024


def _cparams(sem):
    return pltpu.CompilerParams(dimension_semantics=sem, vmem_limit_bytes=VMEM_LIMIT)


def _nt_dot(a, b):
    return lax.dot_general(a, b, (((1,), (1,)), ((), ())), preferred_element_type=F32)


def _dot(a, b):
    return jnp.dot(a, b, preferred_element_type=F32)


def _sigmoid(x):
    return 1.0 / (1.0 + jnp.exp(-x))


def _rms(x, w):
    return x * lax.rsqrt(jnp.mean(x * x, axis=-1, keepdims=True) + EPS) * w


def _inproj_kernel(x_ref, nw_ref, w_ref, b_ref, z_ref, h_scr):
    @pl.when(pl.program_id(1) == 0)
    def _():
        h_scr[...] = _rms(x_ref[...], nw_ref[...]).astype(BF16)

    z_ref[...] = (_dot(h_scr[...], w_ref[...]) + b_ref[...]).astype(BF16)


def _inproj(x2, norm_w, w_in_p, z_bias, tm, tn):
    t = x2.shape[0]
    return pl.pallas_call(
        _inproj_kernel,
        grid=(t // tm, Z_WIDTH // tn),
        in_specs=[
            pl.BlockSpec((tm, D_MODEL), lambda i, j: (i, 0)),
            pl.BlockSpec((1, D_MODEL), lambda i, j: (0, 0)),
            pl.BlockSpec((D_MODEL, tn), lambda i, j: (0, j)),
            pl.BlockSpec((1, tn), lambda i, j: (0, j)),
        ],
        out_specs=pl.BlockSpec((tm, tn), lambda i, j: (i, j)),
        out_shape=jax.ShapeDtypeStruct((t, Z_WIDTH), BF16),
        scratch_shapes=[pltpu.VMEM((tm, D_MODEL), BF16)],
        compiler_params=_cparams(("arbitrary", "arbitrary")),
        name="inproj",
    )(x2, norm_w, w_in_p, z_bias)


def _ret_kernel(q_ref, k_ref, v_ref, g_ref, cos_ref, sin_ref, dm_ref, zeta_ref, xi_ref,
                dec_ref, nw_ref, y_ref, st_scr, *, n_chunks):
    @pl.when(pl.program_id(1) == 0)
    def _():
        st_scr[...] = jnp.zeros_like(st_scr)

    c = RET_CHUNK
    for ci in range(n_chunks):
        rows = slice(ci * c, (ci + 1) * c)
        cos = cos_ref[rows, :]
        sin = sin_ref[rows, :]
        for h in range(RET_HEADS):
            q = q_ref[rows, h * RET_DK:(h + 1) * RET_DK].astype(F32)
            k = k_ref[rows, h * RET_DK:(h + 1) * RET_DK].astype(F32)
            qr = q * cos + pltpu.roll(q, RET_DK // 2, axis=1) * sin
            kr = (k * cos + pltpu.roll(k, RET_DK // 2, axis=1) * sin) * (RET_DK ** -0.5)
            qb = qr.astype(BF16)
            kb = kr.astype(BF16)
            v = v_ref[rows, h * RET_DV:(h + 1) * RET_DV]
            scores = _nt_dot(qb, kb) * dm_ref[h]
            o = _dot(scores.astype(BF16), v)
            st = st_scr[h]
            o = o + _dot(qb, st.astype(BF16)) * xi_ref[h]
            vz = (v.astype(F32) * zeta_ref[h]).astype(BF16)
            kv = _dot(kr.T.astype(BF16), vz)
            st_scr[h] = st * dec_ref[h] + kv
            mu = jnp.mean(o, axis=-1, keepdims=True)
            d = o - mu
            var = jnp.mean(d * d, axis=-1, keepdims=True)
            on = d * lax.rsqrt(var + EPS) * nw_ref[:, h * RET_DV:(h + 1) * RET_DV]
            g = g_ref[rows, h * RET_DV:(h + 1) * RET_DV].astype(F32)
            y_ref[rows, h * RET_DV:(h + 1) * RET_DV] = (g * _sigmoid(g) * on).astype(BF16)


def _retention_consts(s):
    c = RET_CHUNK
    pos = jnp.arange(s, dtype=F32)
    theta = 10000.0 ** (-jnp.linspace(0.0, 1.0, RET_DK // 2, dtype=F32))
    ang = pos[:, None] * theta[None, :]
    cos = jnp.cos(ang)
    sin = jnp.sin(ang)
    cos2 = jnp.concatenate([cos, cos], axis=-1)
    sin2 = jnp.concatenate([-sin, sin], axis=-1)
    log_gamma = jnp.log1p(-(2.0 ** (-5.0 - jnp.arange(RET_HEADS, dtype=F32))))
    j = jnp.arange(c, dtype=F32)
    rel = j[:, None] - j[None, :]
    dmask = jnp.where(rel >= 0, jnp.exp(log_gamma[:, None, None] * jnp.maximum(rel, 0.0)), 0.0)
    zeta = jnp.exp(log_gamma[:, None] * (c - 1 - j)[None, :])
    xi = jnp.exp(log_gamma[:, None] * (j + 1.0)[None, :])
    dec = jnp.exp(log_gamma * c)
    zeta_b = jnp.broadcast_to(zeta[:, :, None], (RET_HEADS, c, RET_DV))
    xi_b = jnp.broadcast_to(xi[:, :, None], (RET_HEADS, c, RET_DV))
    dec_b = jnp.broadcast_to(dec[:, None, None], (RET_HEADS, RET_DK, RET_DV))
    return cos2, sin2, dmask, zeta_b, xi_b, dec_b


def _retention(z, ret_norm_w, consts, b, s, tr):
    t = b * s
    nt = s // tr
    cos2, sin2, dmask, zeta_b, xi_b, dec_b = consts
    const3 = lambda bi, i: (0, 0, 0)
    return pl.pallas_call(
        functools.partial(_ret_kernel, n_chunks=tr // RET_CHUNK),
        grid=(b, nt),
        in_specs=[
            pl.BlockSpec((tr, 512), lambda bi, i: (bi * nt + i, OFF_RQ // 512)),
            pl.BlockSpec((tr, 512), lambda bi, i: (bi * nt + i, OFF_RK // 512)),
            pl.BlockSpec((tr, 1024), lambda bi, i: (bi * nt + i, OFF_RV // 1024)),
            pl.BlockSpec((tr, 1024), lambda bi, i: (bi * nt + i, OFF_RG // 1024)),
            pl.BlockSpec((tr, RET_DK), lambda bi, i: (i, 0)),
            pl.BlockSpec((tr, RET_DK), lambda bi, i: (i, 0)),
            pl.BlockSpec((RET_HEADS, RET_CHUNK, RET_CHUNK), const3),
            pl.BlockSpec((RET_HEADS, RET_CHUNK, RET_DV), const3),
            pl.BlockSpec((RET_HEADS, RET_CHUNK, RET_DV), const3),
            pl.BlockSpec((RET_HEADS, RET_DK, RET_DV), const3),
            pl.BlockSpec((1, D_MODEL), lambda bi, i: (0, 0)),
        ],
        out_specs=pl.BlockSpec((tr, D_MODEL), lambda bi, i: (bi * nt + i, 0)),
        out_shape=jax.ShapeDtypeStruct((t, D_MODEL), BF16),
        scratch_shapes=[pltpu.VMEM((RET_HEADS, RET_DK, RET_DV), F32)],
        compiler_params=_cparams(("arbitrary", "arbitrary")),
        name="retention",
    )(z, z, z, z, cos2, sin2, dmask, zeta_b, xi_b, dec_b, ret_norm_w)


def _compress_kernel(rows_ref, pos_ref, w1_ref, w2_ref, out_ref):
    half = rows_ref.shape[-1]
    x = rows_ref[0, 0].astype(F32)
    lo = _dot((x + pos_ref[0, :, :half]).astype(BF16), w1_ref[0, :half, :])
    hi = _dot((x + pos_ref[0, :, half:]).astype(BF16), w1_ref[0, half:, :])
    hid = lo + pltpu.roll(hi, hi.shape[0] - 1, axis=0)
    gel = 0.5 * hid * (1.0 + jnp.tanh(np.sqrt(2.0 / np.pi) * (hid + 0.044715 * (hid * hid * hid))))
    out_ref[0, 0] = _dot(gel.astype(BF16), w2_ref[0]).astype(BF16)


def _compress(rows, pos_flat, w1, w2p):
    _, bg, ncp, half = rows.shape
    width = 2 * half
    return pl.pallas_call(
        _compress_kernel,
        grid=(2, bg),
        in_specs=[
            pl.BlockSpec((1, 1, ncp, half), lambda a, n: (a, n, 0, 0)),
            pl.BlockSpec((1, 1, width), lambda a, n: (a, 0, 0)),
            pl.BlockSpec((1, width, CMP_HIDDEN), lambda a, n: (a, 0, 0)),
            pl.BlockSpec((1, CMP_HIDDEN, LANES), lambda a, n: (a, 0, 0)),
        ],
        out_specs=pl.BlockSpec((1, 1, ncp, LANES), lambda a, n: (a, n, 0, 0)),
        out_shape=jax.ShapeDtypeStruct((2, bg, ncp, LANES), BF16),
        compiler_params=_cparams(("arbitrary", "arbitrary")),
        name="nsa_compress",
    )(rows, pos_flat, w1, w2p)


def _head_pad(q2_ref_block, h):
    x = q2_ref_block[:, (h // 2) * LANES:(h // 2 + 1) * LANES].astype(F32)
    if h % 2:
        x = pltpu.roll(x, NSA_DH, axis=1)
    lane = lax.broadcasted_iota(jnp.int32, x.shape, 1)
    return jnp.where(lane < NSA_DH, x, 0.0)


def _merge_heads(parts):
    lo = parts[0] + pltpu.roll(parts[1], NSA_DH, axis=1)
    hi = parts[2] + pltpu.roll(parts[3], NSA_DH, axis=1)
    return jnp.concatenate([lo, hi], axis=1)


def _cmp_select_kernel(q_ref, kc_ref, vc_ref, ovt_ref, ocmp_ref, bias_ref, *, tq, n_blk):
    t0 = pl.program_id(1) * tq
    ncp = kc_ref.shape[2]
    kc = kc_ref[0, 0]
    vc = vc_ref[0, 0]
    ovt = ovt_ref[...]
    scale = NSA_DH ** -0.5

    t_q = t0 + lax.broadcasted_iota(jnp.int32, (tq, ncp), 0)
    c_id = lax.broadcasted_iota(jnp.int32, (tq, ncp), 1)
    valid = (c_id * CMP_STRIDE + (CMP_BLOCK - 1)) <= t_q
    t_col = t0 + lax.broadcasted_iota(jnp.int32, (tq, 1), 0)
    any1 = jnp.where(t_col >= CMP_BLOCK - 1, 1.0, 0.0)

    imp = jnp.zeros((n_blk, tq), F32)
    o_parts = []
    for h in range(NSA_HPG):
        qp = (_head_pad(q_ref, h) * scale).astype(BF16)
        s1 = jnp.where(valid, _nt_dot(qp, kc), NEG)
        m = jnp.max(s1, axis=-1, keepdims=True)
        e = jnp.exp(s1 - m)
        pb = (e * ((1.0 / jnp.sum(e, axis=-1, keepdims=True)) * any1)).astype(BF16)
        o_parts.append(_dot(pb, vc))
        imp = imp + _nt_dot(ovt, pb)
    ocmp_ref[...] = _merge_heads(o_parts).astype(BF16)

    blk = lax.broadcasted_iota(jnp.int32, (n_blk, tq), 0)
    t_row = t0 + lax.broadcasted_iota(jnp.int32, (n_blk, tq), 1)
    cur = t_row // SLC_BLOCK
    causal = blk <= cur
    forced = (blk == 0) | (blk == cur) | (blk == cur - 1)
    imp = jnp.where(forced, FORCE, imp)
    imp = jnp.where(causal, imp, NEG)
    sub = 8
    groups = [imp[g * sub:(g + 1) * sub, :] for g in range(n_blk // sub)]
    ranks = [jnp.zeros((sub, tq), F32) for _ in groups]
    row_in_group = lax.broadcasted_iota(jnp.int32, (sub, tq), 0)
    for i in range(n_blk):
        r = jnp.broadcast_to(imp[i:i + 1, :], (sub, tq))
        for g, x in enumerate(groups):
            if g < i // sub:
                ahead = jnp.where(r > x, 1.0, 0.0)
            elif g > i // sub:
                ahead = jnp.where(r >= x, 1.0, 0.0)
            else:
                ahead = jnp.where(row_in_group > i % sub, jnp.where(r >= x, 1.0, 0.0), jnp.where(r > x, 1.0, 0.0))
            ranks[g] = ranks[g] + ahead
    rank = jnp.concatenate(ranks, axis=0)
    bias_t = jnp.where(rank < float(min(N_SELECT, n_blk)), 0.0, NEG)
    pieces = [bias_t]
    if n_blk < NSA_DH:
        pieces.append(jnp.full((NSA_DH - n_blk, tq), NEG, F32))
    pieces.append(jnp.zeros((LANES - NSA_DH, tq), F32))
    bias_ref[0] = jnp.concatenate(pieces, axis=0).T.astype(BF16)


def _cmp_select(z, kvc, ovt, b, s, tq):
    t = b * s
    nq = s // tq
    bg = b * NSA_KV_HEADS
    ncp = kvc.shape[2]
    n_blk = s // SLC_BLOCK
    qcol = OFF_NQ // 256
    return pl.pallas_call(
        functools.partial(_cmp_select_kernel, tq=tq, n_blk=n_blk),
        grid=(bg, nq),
        in_specs=[
            pl.BlockSpec((tq, 256), lambda n, i: ((n // NSA_KV_HEADS) * nq + i, qcol + n % NSA_KV_HEADS)),
            pl.BlockSpec((1, 1, ncp, LANES), lambda n, i: (0, n, 0, 0)),
            pl.BlockSpec((1, 1, ncp, LANES), lambda n, i: (1, n, 0, 0)),
            pl.BlockSpec((n_blk, ncp), lambda n, i: (0, 0)),
        ],
        out_specs=[
            pl.BlockSpec((tq, 256), lambda n, i: ((n // NSA_KV_HEADS) * nq + i, n % NSA_KV_HEADS)),
            pl.BlockSpec((1, tq, LANES), lambda n, i: (n, i, 0)),
        ],
        out_shape=[
            jax.ShapeDtypeStruct((t, D_MODEL), BF16),
            jax.ShapeDtypeStruct((bg, s, LANES), BF16),
        ],
        compiler_params=_cparams(("arbitrary", "arbitrary")),
        name="nsa_cmp_select",
    )(z, kvc, kvc, ovt)


def _normalised_heads(acc_of_head):
    outs = []
    for h in range(NSA_HPG):
        acc = acc_of_head(h)
        lane = lax.broadcasted_iota(jnp.int32, acc.shape, 1)
        denom = jnp.sum(jnp.where(lane == NSA_DH, acc, 0.0), axis=-1, keepdims=True)
        outs.append(jnp.where(lane < NSA_DH, acc * (1.0 / denom), 0.0))
    return _merge_heads(outs).astype(BF16)


def _q_pair(q_ref, h, odd_group):
    x = q_ref[:, (h // 2) * LANES:(h // 2 + 1) * LANES].astype(F32) * (NSA_DH ** -0.5)
    odd_head = jnp.bool_(h % 2 == 1)
    x = jnp.where(odd_head != odd_group, pltpu.roll(x, NSA_DH, axis=1), x)
    lane_half = lax.broadcasted_iota(jnp.int32, x.shape, 1) // NSA_DH
    return jnp.where(lane_half == odd_group.astype(jnp.int32), x, 0.0).astype(BF16)


def _flash_sel_kernel(q_ref, b_ref, k_ref, oh_ref, v_ref, o_ref, qa_scr, m_scr, acc_scr, *, tq, tk, td):
    t0 = pl.program_id(1) * tq
    odd_group = (pl.program_id(0) % 2) == 1
    for h in range(NSA_HPG):
        qa_scr[h] = jnp.concatenate([_q_pair(q_ref, h, odd_group), b_ref[0]], axis=1)
    m_scr[...] = jnp.full(m_scr.shape, NEG, F32)
    acc_scr[...] = jnp.zeros(acc_scr.shape, F32)

    def tile(start, row0, width, masked):
        k = jnp.concatenate([k_ref[pl.ds(start, width), :], oh_ref[pl.ds(start, width), :]], axis=1)
        v = v_ref[pl.ds(start, width), :]
        nr = tq - row0
        if masked:
            t_q = t0 + row0 + lax.broadcasted_iota(jnp.int32, (nr, width), 0)
            ok = (start + lax.broadcasted_iota(jnp.int32, (nr, width), 1)) <= t_q
        scores = lambda h: _nt_dot(qa_scr[h, row0:tq, :], k)

        sc_next = scores(0)
        for h in range(NSA_HPG):
            sc = sc_next
            if h + 1 < NSA_HPG:
                sc_next = scores(h + 1)
            if masked:
                sc = jnp.where(ok, sc, NEG)
            m_prev = m_scr[h, row0:tq, :]
            m_new = jnp.maximum(m_prev, jnp.max(sc, axis=-1, keepdims=True))
            alpha = jnp.exp(m_prev - m_new)
            p = jnp.exp(sc - jnp.concatenate([m_new] * (width // LANES), axis=1))
            acc_scr[h, row0:tq, :] = alpha * acc_scr[h, row0:tq, :] + _dot(p.astype(BF16), v)
            m_scr[h, row0:tq, :] = m_new

    def body(kt, carry):
        tile(pl.multiple_of(kt * tk, tk), 0, tk, False)
        return carry

    lax.fori_loop(0, t0 // tk, body, 0)
    for j in range(tq // td):
        tile(pl.multiple_of(t0 + j * td, td), j * td, td, True)
    o_ref[...] = _normalised_heads(lambda h: acc_scr[h])


def _flash_win_kernel(q_ref, k_ref, v_ref, band_ref, o_ref, *, tq):
    t0 = pl.program_id(1) * tq
    odd_group = (pl.program_id(0) % 2) == 1
    n_tiles = WINDOW // tq + 1
    ks, vs, biases = [], [], []
    for j in range(n_tiles):
        first = t0 + (j - n_tiles + 1) * tq
        start = pl.multiple_of(jnp.maximum(first, 0), tq)
        ks.append(k_ref[pl.ds(start, tq), :])
        vs.append(v_ref[pl.ds(start, tq), :])
        biases.append(jnp.where(first < 0, NEG, band_ref[j]))

    def scores(h0):
        q = jnp.concatenate([_q_pair(q_ref, h0 + d, odd_group) for d in range(WIN_STACK)], axis=0)
        return [_nt_dot(q, ks[j]) + jnp.concatenate([biases[j]] * WIN_STACK, axis=0) for j in range(n_tiles)]

    accs = []
    scs_next = scores(0)
    for h0 in range(0, NSA_HPG, WIN_STACK):
        scs = scs_next
        if h0 + WIN_STACK < NSA_HPG:
            scs_next = scores(h0 + WIN_STACK)
        m = jnp.max(scs[0], axis=-1, keepdims=True)
        for sc in scs[1:]:
            m = jnp.maximum(m, jnp.max(sc, axis=-1, keepdims=True))
        acc = _dot(jnp.exp(scs[0] - m).astype(BF16), vs[0])
        for j in range(1, n_tiles):
            acc = acc + _dot(jnp.exp(scs[j] - m).astype(BF16), vs[j])
        accs.extend(acc[d * tq:(d + 1) * tq] for d in range(WIN_STACK))

    o_ref[...] = _normalised_heads(lambda h: accs[h])


def _flash(z, sel_bias, blk_onehot, b, s, tq, tk, td, mode):
    t = b * s
    nq = s // tq
    bg = b * NSA_KV_HEADS
    grp = lambda n: n % NSA_KV_HEADS
    bat = lambda n: n // NSA_KV_HEADS
    off_k, off_v = (OFF_KS, OFF_VS) if mode == "sel" else (OFF_KW, OFF_VW)
    q_spec = pl.BlockSpec((tq, 256), lambda n, i: (bat(n) * nq + i, OFF_NQ // 256 + grp(n)))
    k_spec = pl.BlockSpec((s, LANES), lambda n, i: (bat(n), off_k // LANES + grp(n) // 2))
    v_spec = pl.BlockSpec((s, LANES), lambda n, i: (bat(n), off_v // LANES + grp(n)))
    if mode == "sel":
        body = functools.partial(_flash_sel_kernel, tq=tq, tk=tk, td=td)
        in_specs = [q_spec, pl.BlockSpec((1, tq, LANES), lambda n, i: (n, i, 0)), k_spec,
                    pl.BlockSpec((s, LANES), lambda n, i: (0, 0)), v_spec]
        args = (z, sel_bias, z, blk_onehot, z)
        scratch = [pltpu.VMEM((NSA_HPG, tq, 2 * LANES), BF16), pltpu.VMEM((NSA_HPG, tq, LANES), F32),
                   pltpu.VMEM((NSA_HPG, tq, LANES), F32)]
    else:
        body = functools.partial(_flash_win_kernel, tq=tq)
        n_tiles = WINDOW // tq + 1
        rel = (np.arange(tq)[None, :, None] - np.arange(tq)[None, None, :]
               + (n_tiles - 1 - np.arange(n_tiles))[:, None, None] * tq)
        band = jnp.asarray(np.where((rel >= 0) & (rel < WINDOW), 0.0, NEG), F32)
        in_specs = [q_spec, k_spec, v_spec, pl.BlockSpec((n_tiles, tq, tq), lambda n, i: (0, 0, 0))]
        args = (z, z, z, band)
        scratch = []
    return pl.pallas_call(
        body,
        grid=(bg, nq),
        in_specs=in_specs,
        out_specs=pl.BlockSpec((tq, 256), lambda n, i: (bat(n) * nq + i, grp(n))),
        out_shape=jax.ShapeDtypeStruct((t, D_MODEL), BF16),
        scratch_shapes=scratch,
        compiler_params=_cparams(("arbitrary", "arbitrary")),
        name="nsa_flash_" + mode,
    )(*args)


def _shifted(u, prev, n):
    rolled = pltpu.roll(u, n, axis=0)
    head = rolled[:HALO_ROWS]
    row = lax.broadcasted_iota(jnp.int32, head.shape, 0)
    for r in range(n):
        head = jnp.where(row == r, prev[HALO_ROWS - n + r:HALO_ROWS - n + r + 1, :], head)
    return jnp.concatenate([head, rolled[HALO_ROWS:]], axis=0)


def _causal_conv3(u, prev, w):
    return w[0:1, :] * _shifted(u, prev, 2) + w[1:2, :] * _shifted(u, prev, 1) + w[2:3, :] * u


def _merge_kernel(x_ref, yret_ref, sb_ref, sc_ref, sx_ref, sch_ref, sxh_ref, mg0_ref, mg1_ref, mg2_ref,
                  ng_ref, ocmp_ref, oslc_ref, owin_ref, cw_ref, e_ref, wret_ref, wsc_ref, wnsa_ref,
                  wmix_ref, out_ref, *, tiles_per_seq):
    first = (pl.program_id(0) % tiles_per_seq) == 0
    u = sc_ref[...].astype(F32) * sx_ref[...].astype(F32)
    prev = sch_ref[...].astype(F32) * sxh_ref[...].astype(F32)
    prev = jnp.where(first, 0.0, prev)
    y_sc = (sb_ref[...].astype(F32) * _causal_conv3(u, prev, cw_ref[...])).astype(BF16)

    gates = _dot(_sigmoid(ng_ref[...].astype(F32)).astype(BF16), e_ref[...])
    y_nsa = (gates[:, 0:D_MODEL] * ocmp_ref[...].astype(F32)
             + gates[:, D_MODEL:2 * D_MODEL] * oslc_ref[...].astype(F32)
             + gates[:, 2 * D_MODEL:3 * D_MODEL] * owin_ref[...].astype(F32)).astype(BF16)

    merged = (_sigmoid(mg0_ref[...].astype(F32)) * _dot(yret_ref[...], wret_ref[...])
              + _sigmoid(mg1_ref[...].astype(F32)) * _dot(y_sc, wsc_ref[...])
              + _sigmoid(mg2_ref[...].astype(F32)) * _dot(y_nsa, wnsa_ref[...]))
    out_ref[...] = x_ref[...] + _dot(merged.astype(BF16), wmix_ref[...])


def _merge(x2, z, y_ret, o_cmp, o_slc, o_win, conv_w, gate_expand, w_ret, w_sc, w_nsa, w_mix, s, tm):
    t = x2.shape[0]
    row = lambda c: (lambda i: (i, c))
    halo = lambda c: (lambda i: (jnp.maximum(i * (tm // HALO_ROWS) - 1, 0), c))
    const = lambda i: (0, 0)
    wspec = pl.BlockSpec((D_MODEL, D_MODEL), const, pipeline_mode=pl.Buffered(1))
    return pl.pallas_call(
        functools.partial(_merge_kernel, tiles_per_seq=s // tm),
        grid=(t // tm,),
        in_specs=[
            pl.BlockSpec((tm, D_MODEL), row(0)),
            pl.BlockSpec((tm, D_MODEL), row(0)),
            pl.BlockSpec((tm, D_MODEL), row(OFF_SB // D_MODEL)),
            pl.BlockSpec((tm, D_MODEL), row(OFF_SC // D_MODEL)),
            pl.BlockSpec((tm, D_MODEL), row(OFF_SX // D_MODEL)),
            pl.BlockSpec((HALO_ROWS, D_MODEL), halo(OFF_SC // D_MODEL)),
            pl.BlockSpec((HALO_ROWS, D_MODEL), halo(OFF_SX // D_MODEL)),
            pl.BlockSpec((tm, D_MODEL), row(OFF_MG // D_MODEL)),
            pl.BlockSpec((tm, D_MODEL), row(OFF_MG // D_MODEL + 1)),
            pl.BlockSpec((tm, D_MODEL), row(OFF_MG // D_MODEL + 2)),
            pl.BlockSpec((tm, LANES), row(OFF_NG // LANES)),
            pl.BlockSpec((tm, D_MODEL), row(0)),
            pl.BlockSpec((tm, D_MODEL), row(0)),
            pl.BlockSpec((tm, D_MODEL), row(0)),
            pl.BlockSpec((CONV_WIDTH, D_MODEL), const),
            pl.BlockSpec((LANES, 3 * D_MODEL), const),
            wspec, wspec, wspec, wspec,
        ],
        out_specs=pl.BlockSpec((tm, D_MODEL), row(0)),
        out_shape=jax.ShapeDtypeStruct((t, D_MODEL), F32),
        compiler_params=_cparams(("arbitrary",)),
        name="merge",
    )(x2, y_ret, z, z, z, z, z, z, z, z, z, o_cmp, o_slc, o_win, conv_w, gate_expand,
      w_ret, w_sc, w_nsa, w_mix)


def _ffn_kernel(x_ref, nw_ref, wup_ref, cw_ref, wd_ref, fw_ref, out_ref, h_scr, ca_scr, cv_scr,
                *, tiles_per_seq, final_norm, tf):
    @pl.when((pl.program_id(0) % tiles_per_seq) == 0)
    def _():
        ca_scr[...] = jnp.zeros_like(ca_scr)
        cv_scr[...] = jnp.zeros_like(cv_scr)

    h_scr[...] = _rms(x_ref[...], nw_ref[...]).astype(BF16)
    out_ref[...] = x_ref[...]
    tm = h_scr.shape[0]
    nf = D_FF // tf
    cols_a = lambda j: slice(j * tf, (j + 1) * tf)
    cols_v = lambda j: slice(D_FF + j * tf, D_FF + (j + 1) * tf)

    def up(j):
        h = h_scr[...]
        return _dot(h, wup_ref[:, cols_a(j)]), _dot(h, wup_ref[:, cols_v(j)])

    def gate(j, ua, uv):
        a = _causal_conv3(ua, ca_scr[j], cw_ref[:, cols_a(j)])
        v = _causal_conv3(uv, cv_scr[j], cw_ref[:, cols_v(j)])
        ca_scr[j] = ua[tm - HALO_ROWS:, :]
        cv_scr[j] = uv[tm - HALO_ROWS:, :]
        return (a * _sigmoid(a) * v).astype(BF16)

    u_next = up(0)
    act_prev = None
    for j in range(nf):
        u_cur = u_next
        if j + 1 < nf:
            u_next = up(j + 1)
        act = gate(j, *u_cur)
        if act_prev is not None:
            out_ref[...] += _dot(act_prev, wd_ref[cols_a(j - 1), :])
        act_prev = act
    out_ref[...] += _dot(act_prev, wd_ref[cols_a(nf - 1), :])
    if final_norm:
        out_ref[...] = _rms(out_ref[...], fw_ref[...])


def _ffn(x2, norm_w, w_up, conv_w, w_down, final_w, s, tm, tf, final_norm):
    t = x2.shape[0]
    nf = D_FF // tf
    const = lambda i: (0, 0)
    resident = dict(index_map=const, pipeline_mode=pl.Buffered(1))
    return pl.pallas_call(
        functools.partial(_ffn_kernel, tiles_per_seq=s // tm, final_norm=final_norm, tf=tf),
        grid=(t // tm,),
        in_specs=[
            pl.BlockSpec((tm, D_MODEL), lambda i: (i, 0)),
            pl.BlockSpec((1, D_MODEL), const),
            pl.BlockSpec((D_MODEL, 2 * D_FF), **resident),
            pl.BlockSpec((CONV_WIDTH, 2 * D_FF), const),
            pl.BlockSpec((D_FF, D_MODEL), **resident),
            pl.BlockSpec((1, D_MODEL), const),
        ],
        out_specs=pl.BlockSpec((tm, D_MODEL), lambda i: (i, 0)),
        out_shape=jax.ShapeDtypeStruct((t, D_MODEL), F32),
        scratch_shapes=[
            pltpu.VMEM((tm, D_MODEL), BF16),
            pltpu.VMEM((nf, HALO_ROWS, tf), F32),
            pltpu.VMEM((nf, HALO_ROWS, tf), F32),
        ],
        compiler_params=_cparams(("arbitrary",)),
        name="ffn",
    )(x2, norm_w, w_up, conv_w, w_down, final_w)


def _prep_w_in(w_in_l):
    w_in_l = w_in_l.astype(BF16)
    gw = NSA_KV_HEADS * NSA_DH
    src = OFF_MG
    k_c, v_c, k_s, v_s, k_w, v_w = [w_in_l[:, src + n * gw: src + (n + 1) * gw] for n in range(6)]
    ng = w_in_l[:, src + 6 * gw: src + 6 * gw + 3 * NSA_HEADS]
    ng = ng.reshape(D_MODEL, NSA_HEADS, 3).transpose(0, 2, 1).reshape(D_MODEL, 3 * NSA_HEADS)
    mg = w_in_l[:, src + 6 * gw + 3 * NSA_HEADS:]

    def lane_pad(v):
        v = v.reshape(D_MODEL, NSA_KV_HEADS, NSA_DH)
        return jnp.pad(v, ((0, 0), (0, 0), (0, LANES - NSA_DH))).reshape(D_MODEL, NSA_KV_HEADS * LANES)

    pad = jnp.zeros((D_MODEL, Z_WIDTH - OFF_NG - 3 * NSA_HEADS), w_in_l.dtype)
    return jnp.concatenate([w_in_l[:, :OFF_MG], mg, k_c, v_c, k_s, k_w, lane_pad(v_s), lane_pad(v_w), ng, pad],
                           axis=1).astype(BF16)


def _z_bias():
    bias = np.zeros((1, Z_WIDTH), np.float32)
    for off in (OFF_VS, OFF_VW):
        for g in range(NSA_KV_HEADS):
            bias[0, off + g * LANES + NSA_DH] = 1.0
    return jnp.asarray(bias)


def _gate_expander():
    e = np.zeros((LANES, 3 * D_MODEL), np.float32)
    for br in range(3):
        for h in range(NSA_HEADS):
            e[br * NSA_HEADS + h, br * D_MODEL + h * NSA_DH: br * D_MODEL + (h + 1) * NSA_DH] = 1.0
    return jnp.asarray(e, BF16)


def _overlap_t(s, ncp):
    nc = (s - CMP_BLOCK) // CMP_STRIDE + 1
    ns = s // SLC_BLOCK
    ci = np.arange(ncp) * CMP_STRIDE
    sj = np.arange(ns) * SLC_BLOCK
    ov = ((ci[None, :] < sj[:, None] + SLC_BLOCK) & (ci[None, :] + CMP_BLOCK > sj[:, None])
          & (np.arange(ncp)[None, :] < nc))
    return jnp.asarray(ov.astype(np.float32), BF16)


def kernel(x, attn_norm_w, w_in, ret_norm_w, w_ret_out, sc_conv_w, w_sc_out, nsa_cmp_pos, nsa_cmp_w1,
           nsa_cmp_w2, w_nsa_out, w_mix_out, ffn_norm_w, w_ffn_up, ffn_conv_w, w_ffn_down, final_norm_w):
    b, s, _ = x.shape
    t = b * s
    depth = w_in.shape[0]
    bg = b * NSA_KV_HEADS
    ncp = s // CMP_STRIDE
    assert s % 256 == 0 and s // SLC_BLOCK <= NSA_DH

    tm_in = min(1024, t)
    tm_merge = 512
    tm_ffn = min(1024, s)
    tq_cmp = 512
    tq_sel = 1024
    tk_sel = 512
    td_sel = 512
    tq_win = 256
    tr = 256

    ret_consts = _retention_consts(s)
    gate_expand = _gate_expander()
    z_bias = _z_bias()
    ovt = _overlap_t(s, ncp)
    onehot = np.zeros((s, LANES), np.float32)
    onehot[np.arange(s), (np.arange(s) // SLC_BLOCK) % NSA_DH] = 1.0
    blk_onehot = jnp.asarray(onehot, BF16)

    x2 = x.reshape(t, D_MODEL)
    for l in range(depth):
        z = _inproj(x2, attn_norm_w[l][None], _prep_w_in(w_in[l]), z_bias, tm_in, Z_TILE)
        y_ret = _retention(z, ret_norm_w[l][None], ret_consts, b, s, tr)

        kv_c = z[:, OFF_KC:OFF_KC + 2 * NSA_KV_HEADS * NSA_DH]
        kv_c = kv_c.reshape(b, ncp, CMP_STRIDE, 2, NSA_KV_HEADS, NSA_DH).transpose(3, 0, 4, 1, 2, 5)
        kv_c = kv_c.reshape(2, bg, ncp, CMP_STRIDE * NSA_DH)
        pos_flat = nsa_cmp_pos[l].reshape(2, 1, CMP_BLOCK * NSA_DH)
        w2p = jnp.pad(nsa_cmp_w2[l], ((0, 0), (0, 0), (0, LANES - NSA_DH))).astype(BF16)
        kvc = _compress(kv_c, pos_flat, nsa_cmp_w1[l].astype(BF16), w2p)

        o_cmp, sel_bias = _cmp_select(z, kvc, ovt, b, s, tq_cmp)
        o_slc = _flash(z, sel_bias, blk_onehot, b, s, tq_sel, tk_sel, td_sel, "sel")
        o_win = _flash(z, None, None, b, s, tq_win, None, None, "win")

        x2 = _merge(x2, z, y_ret, o_cmp, o_slc, o_win, sc_conv_w[l], gate_expand,
                    w_ret_out[l].astype(BF16), w_sc_out[l].astype(BF16), w_nsa_out[l].astype(BF16),
                    w_mix_out[l].astype(BF16), s, tm_merge)
        x2 = _ffn(x2, ffn_norm_w[l][None], w_ffn_up[l].astype(BF16), ffn_conv_w[l],
                  w_ffn_down[l].astype(BF16), final_norm_w[None], s, tm_ffn, 256, l == depth - 1)
    return x2.reshape(b, s, D_MODEL)
```

```python
import functools

import numpy as np
import jax
import jax.numpy as jnp
from jax import lax
from jax.experimental import pallas as pl
from jax.experimental.pallas import tpu as pltpu

F32 = jnp.float32
BF16 = jnp.bfloat16

D_MODEL = 1024
RET_HEADS = 4
RET_DV = D_MODEL // RET_HEADS
RET_DK = RET_DV // 2
RET_CHUNK = 128
CONV_WIDTH = 3
NSA_DH = 64
NSA_HEADS = D_MODEL // NSA_DH
NSA_KV_HEADS = 4
NSA_HPG = NSA_HEADS // NSA_KV_HEADS
CMP_BLOCK = 32
CMP_STRIDE = 16
CMP_HIDDEN = 256
SLC_BLOCK = 64
N_SELECT = 16
WINDOW = 512
D_FF = ((8 * D_MODEL // 3 + 127) // 128) * 128
EPS = 1e-6
NEG = -1e30
FORCE = 1e6

WIN_STACK = 2
LANES = 128
HALO_ROWS = 8

OFF_RQ = 0
OFF_RK = 512
OFF_RV = 1024
OFF_RG = 2048
OFF_SB = 3072
OFF_SC = 4096
OFF_SX = 5120
OFF_NQ = 6144
OFF_MG = 7168
OFF_KC = 10240
OFF_VC = 10496
OFF_KS = 10752
OFF_KW = 11008
OFF_VS = 11264
OFF_VW = 11776
OFF_NG = 12288
Z_WIDTH = 12800
Z_TILE = 1280

VMEM_LIMIT = 56 * 1024 * 1024


def _cparams(sem):
    return pltpu.CompilerParams(dimension_semantics=sem, vmem_limit_bytes=VMEM_LIMIT)


def _nt_dot(a, b):
    return lax.dot_general(a, b, (((1,), (1,)), ((), ())), preferred_element_type=F32)


def _dot(a, b):
    return jnp.dot(a, b, preferred_element_type=F32)


def _sigmoid(x):
    return 1.0 / (1.0 + jnp.exp(-x))


def _rms(x, w):
    return x * lax.rsqrt(jnp.mean(x * x, axis=-1, keepdims=True) + EPS) * w


def _inproj_kernel(x_ref, nw_ref, w_ref, b_ref, z_ref, h_scr):
    @pl.when(pl.program_id(1) == 0)
    def _():
        h_scr[...] = _rms(x_ref[...], nw_ref[...]).astype(BF16)

    z_ref[...] = (_dot(h_scr[...], w_ref[...]) + b_ref[...]).astype(BF16)


def _inproj(x2, norm_w, w_in_p, z_bias, tm, tn):
    t = x2.shape[0]
    return pl.pallas_call(
        _inproj_kernel,
        grid=(t // tm, Z_WIDTH // tn),
        in_specs=[
            pl.BlockSpec((tm, D_MODEL), lambda i, j: (i, 0)),
            pl.BlockSpec((1, D_MODEL), lambda i, j: (0, 0)),
            pl.BlockSpec((D_MODEL, tn), lambda i, j: (0, j)),
            pl.BlockSpec((1, tn), lambda i, j: (0, j)),
        ],
        out_specs=pl.BlockSpec((tm, tn), lambda i, j: (i, j)),
        out_shape=jax.ShapeDtypeStruct((t, Z_WIDTH), BF16),
        scratch_shapes=[pltpu.VMEM((tm, D_MODEL), BF16)],
        compiler_params=_cparams(("arbitrary", "arbitrary")),
        name="inproj",
    )(x2, norm_w, w_in_p, z_bias)


def _ret_kernel(q_ref, k_ref, v_ref, g_ref, cos_ref, sin_ref, dm_ref, zeta_ref, xi_ref,
                dec_ref, nw_ref, y_ref, st_scr, *, n_chunks):
    @pl.when(pl.program_id(1) == 0)
    def _():
        st_scr[...] = jnp.zeros_like(st_scr)

    c = RET_CHUNK
    for ci in range(n_chunks):
        rows = slice(ci * c, (ci + 1) * c)
        cos = cos_ref[rows, :]
        sin = sin_ref[rows, :]
        for h in range(RET_HEADS):
            q = q_ref[rows, h * RET_DK:(h + 1) * RET_DK].astype(F32)
            k = k_ref[rows, h * RET_DK:(h + 1) * RET_DK].astype(F32)
            qr = q * cos + pltpu.roll(q, RET_DK // 2, axis=1) * sin
            kr = (k * cos + pltpu.roll(k, RET_DK // 2, axis=1) * sin) * (RET_DK ** -0.5)
            qb = qr.astype(BF16)
            kb = kr.astype(BF16)
            v = v_ref[rows, h * RET_DV:(h + 1) * RET_DV]
            scores = _nt_dot(qb, kb) * dm_ref[h]
            o = _dot(scores.astype(BF16), v)
            st = st_scr[h]
            o = o + _dot(qb, st.astype(BF16)) * xi_ref[h]
            vz = (v.astype(F32) * zeta_ref[h]).astype(BF16)
            kv = _dot(kr.T.astype(BF16), vz)
            st_scr[h] = st * dec_ref[h] + kv
            mu = jnp.mean(o, axis=-1, keepdims=True)
            d = o - mu
            var = jnp.mean(d * d, axis=-1, keepdims=True)
            on = d * lax.rsqrt(var + EPS) * nw_ref[:, h * RET_DV:(h + 1) * RET_DV]
            g = g_ref[rows, h * RET_DV:(h + 1) * RET_DV].astype(F32)
            y_ref[rows, h * RET_DV:(h + 1) * RET_DV] = (g * _sigmoid(g) * on).astype(BF16)


def _retention_consts(s):
    c = RET_CHUNK
    pos = jnp.arange(s, dtype=F32)
    theta = 10000.0 ** (-jnp.linspace(0.0, 1.0, RET_DK // 2, dtype=F32))
    ang = pos[:, None] * theta[None, :]
    cos = jnp.cos(ang)
    sin = jnp.sin(ang)
    cos2 = jnp.concatenate([cos, cos], axis=-1)
    sin2 = jnp.concatenate([-sin, sin], axis=-1)
    log_gamma = jnp.log1p(-(2.0 ** (-5.0 - jnp.arange(RET_HEADS, dtype=F32))))
    j = jnp.arange(c, dtype=F32)
    rel = j[:, None] - j[None, :]
    dmask = jnp.where(rel >= 0, jnp.exp(log_gamma[:, None, None] * jnp.maximum(rel, 0.0)), 0.0)
    zeta = jnp.exp(log_gamma[:, None] * (c - 1 - j)[None, :])
    xi = jnp.exp(log_gamma[:, None] * (j + 1.0)[None, :])
    dec = jnp.exp(log_gamma * c)
    zeta_b = jnp.broadcast_to(zeta[:, :, None], (RET_HEADS, c, RET_DV))
    xi_b = jnp.broadcast_to(xi[:, :, None], (RET_HEADS, c, RET_DV))
    dec_b = jnp.broadcast_to(dec[:, None, None], (RET_HEADS, RET_DK, RET_DV))
    return cos2, sin2, dmask, zeta_b, xi_b, dec_b


def _retention(z, ret_norm_w, consts, b, s, tr):
    t = b * s
    nt = s // tr
    cos2, sin2, dmask, zeta_b, xi_b, dec_b = consts
    const3 = lambda bi, i: (0, 0, 0)
    return pl.pallas_call(
        functools.partial(_ret_kernel, n_chunks=tr // RET_CHUNK),
        grid=(b, nt),
        in_specs=[
            pl.BlockSpec((tr, 512), lambda bi, i: (bi * nt + i, OFF_RQ // 512)),
            pl.BlockSpec((tr, 512), lambda bi, i: (bi * nt + i, OFF_RK // 512)),
            pl.BlockSpec((tr, 1024), lambda bi, i: (bi * nt + i, OFF_RV // 1024)),
            pl.BlockSpec((tr, 1024), lambda bi, i: (bi * nt + i, OFF_RG // 1024)),
            pl.BlockSpec((tr, RET_DK), lambda bi, i: (i, 0)),
            pl.BlockSpec((tr, RET_DK), lambda bi, i: (i, 0)),
            pl.BlockSpec((RET_HEADS, RET_CHUNK, RET_CHUNK), const3),
            pl.BlockSpec((RET_HEADS, RET_CHUNK, RET_DV), const3),
            pl.BlockSpec((RET_HEADS, RET_CHUNK, RET_DV), const3),
            pl.BlockSpec((RET_HEADS, RET_DK, RET_DV), const3),
            pl.BlockSpec((1, D_MODEL), lambda bi, i: (0, 0)),
        ],
        out_specs=pl.BlockSpec((tr, D_MODEL), lambda bi, i: (bi * nt + i, 0)),
        out_shape=jax.ShapeDtypeStruct((t, D_MODEL), BF16),
        scratch_shapes=[pltpu.VMEM((RET_HEADS, RET_DK, RET_DV), F32)],
        compiler_params=_cparams(("arbitrary", "arbitrary")),
        name="retention",
    )(z, z, z, z, cos2, sin2, dmask, zeta_b, xi_b, dec_b, ret_norm_w)


def _compress_kernel(rows_ref, pos_ref, w1_ref, w2_ref, out_ref):
    half = rows_ref.shape[-1]
    x = rows_ref[0, 0].astype(F32)
    lo = _dot((x + pos_ref[0, :, :half]).astype(BF16), w1_ref[0, :half, :])
    hi = _dot((x + pos_ref[0, :, half:]).astype(BF16), w1_ref[0, half:, :])
    hid = lo + pltpu.roll(hi, hi.shape[0] - 1, axis=0)
    gel = 0.5 * hid * (1.0 + jnp.tanh(np.sqrt(2.0 / np.pi) * (hid + 0.044715 * (hid * hid * hid))))
    out_ref[0, 0] = _dot(gel.astype(BF16), w2_ref[0]).astype(BF16)


def _compress(rows, pos_flat, w1, w2p):
    _, bg, ncp, half = rows.shape
    width = 2 * half
    return pl.pallas_call(
        _compress_kernel,
        grid=(2, bg),
        in_specs=[
            pl.BlockSpec((1, 1, ncp, half), lambda a, n: (a, n, 0, 0)),
            pl.BlockSpec((1, 1, width), lambda a, n: (a, 0, 0)),
            pl.BlockSpec((1, width, CMP_HIDDEN), lambda a, n: (a, 0, 0)),
            pl.BlockSpec((1, CMP_HIDDEN, LANES), lambda a, n: (a, 0, 0)),
        ],
        out_specs=pl.BlockSpec((1, 1, ncp, LANES), lambda a, n: (a, n, 0, 0)),
        out_shape=jax.ShapeDtypeStruct((2, bg, ncp, LANES), BF16),
        compiler_params=_cparams(("arbitrary", "arbitrary")),
        name="nsa_compress",
    )(rows, pos_flat, w1, w2p)


def _head_pad(q2_ref_block, h):
    x = q2_ref_block[:, (h // 2) * LANES:(h // 2 + 1) * LANES].astype(F32)
    if h % 2:
        x = pltpu.roll(x, NSA_DH, axis=1)
    lane = lax.broadcasted_iota(jnp.int32, x.shape, 1)
    return jnp.where(lane < NSA_DH, x, 0.0)


def _merge_heads(parts):
    lo = parts[0] + pltpu.roll(parts[1], NSA_DH, axis=1)
    hi = parts[2] + pltpu.roll(parts[3], NSA_DH, axis=1)
    return jnp.concatenate([lo, hi], axis=1)


def _cmp_select_kernel(q_ref, kc_ref, vc_ref, ovt_ref, ocmp_ref, bias_ref, *, tq, n_blk):
    for ti in range(q_ref.shape[0] // tq):
        _cmp_select_tile(ti * tq, q_ref, kc_ref, vc_ref, ovt_ref, ocmp_ref, bias_ref, tq, n_blk)


def _cmp_select_tile(t0, q_ref, kc_ref, vc_ref, ovt_ref, ocmp_ref, bias_ref, tq, n_blk_all):
    rows = slice(t0, t0 + tq)
    ncp = min(kc_ref.shape[2], -(-((t0 + tq) // CMP_STRIDE) // LANES) * LANES)
    n_blk = min(n_blk_all, -(-((t0 + tq) // SLC_BLOCK) // 8) * 8)
    kc = kc_ref[0, 0, :ncp, :]
    vc = vc_ref[0, 0, :ncp, :]
    ovt = ovt_ref[:n_blk, :ncp]
    scale = NSA_DH ** -0.5

    t_q = t0 + lax.broadcasted_iota(jnp.int32, (tq, ncp), 0)
    c_id = lax.broadcasted_iota(jnp.int32, (tq, ncp), 1)
    valid = (c_id * CMP_STRIDE + (CMP_BLOCK - 1)) <= t_q
    t_col = t0 + lax.broadcasted_iota(jnp.int32, (tq, 1), 0)
    any1 = jnp.where(t_col >= CMP_BLOCK - 1, 1.0, 0.0)

    imp = jnp.zeros((n_blk, tq), F32)
    o_parts = []
    for h in range(NSA_HPG):
        qp = (_head_pad(q_ref.at[rows, :], h) * scale).astype(BF16)
        s1 = jnp.where(valid, _nt_dot(qp, kc), NEG)
        m = jnp.max(s1, axis=-1, keepdims=True)
        e = jnp.exp(s1 - m)
        pb = (e * ((1.0 / jnp.sum(e, axis=-1, keepdims=True)) * any1)).astype(BF16)
        o_parts.append(_dot(pb, vc))
        imp = imp + _nt_dot(ovt, pb)
    ocmp_ref[rows, :] = _merge_heads(o_parts).astype(BF16)

    blk = lax.broadcasted_iota(jnp.int32, (n_blk, tq), 0)
    t_row = t0 + lax.broadcasted_iota(jnp.int32, (n_blk, tq), 1)
    cur = t_row // SLC_BLOCK
    causal = blk <= cur
    forced = (blk == 0) | (blk == cur) | (blk == cur - 1)
    imp = jnp.where(forced, FORCE, imp)
    imp = jnp.where(causal, imp, NEG)
    sub = 8
    groups = [imp[g * sub:(g + 1) * sub, :] for g in range(n_blk // sub)]
    ranks = [jnp.zeros((sub, tq), F32) for _ in groups]
    row_in_group = lax.broadcasted_iota(jnp.int32, (sub, tq), 0)
    for i in range(n_blk):
        r = jnp.broadcast_to(imp[i:i + 1, :], (sub, tq))
        for g, x in enumerate(groups):
            if g < i // sub:
                ahead = jnp.where(r > x, 1.0, 0.0)
            elif g > i // sub:
                ahead = jnp.where(r >= x, 1.0, 0.0)
            else:
                ahead = jnp.where(row_in_group > i % sub, jnp.where(r >= x, 1.0, 0.0), jnp.where(r > x, 1.0, 0.0))
            ranks[g] = ranks[g] + ahead
    rank = jnp.concatenate(ranks, axis=0)
    bias_t = jnp.where(rank < float(min(N_SELECT, n_blk_all)), 0.0, NEG)
    pieces = [bias_t]
    if n_blk < NSA_DH:
        pieces.append(jnp.full((NSA_DH - n_blk, tq), NEG, F32))
    pieces.append(jnp.zeros((LANES - NSA_DH, tq), F32))
    bias_ref[0, rows, :] = jnp.concatenate(pieces, axis=0).T.astype(BF16)


def _cmp_select(z, kvc, ovt, b, s, tq):
    t = b * s
    bg = b * NSA_KV_HEADS
    ncp = kvc.shape[2]
    n_blk = s // SLC_BLOCK
    qcol = OFF_NQ // 256
    return pl.pallas_call(
        functools.partial(_cmp_select_kernel, tq=tq, n_blk=n_blk),
        grid=(bg,),
        in_specs=[
            pl.BlockSpec((s, 256), lambda n: (n // NSA_KV_HEADS, qcol + n % NSA_KV_HEADS)),
            pl.BlockSpec((1, 1, ncp, LANES), lambda n: (0, n, 0, 0)),
            pl.BlockSpec((1, 1, ncp, LANES), lambda n: (1, n, 0, 0)),
            pl.BlockSpec((n_blk, ncp), lambda n: (0, 0)),
        ],
        out_specs=[
            pl.BlockSpec((s, 256), lambda n: (n // NSA_KV_HEADS, n % NSA_KV_HEADS)),
            pl.BlockSpec((1, s, LANES), lambda n: (n, 0, 0)),
        ],
        out_shape=[
            jax.ShapeDtypeStruct((t, D_MODEL), BF16),
            jax.ShapeDtypeStruct((bg, s, LANES), BF16),
        ],
        compiler_params=_cparams(("arbitrary",)),
        name="nsa_cmp_select",
    )(z, kvc, kvc, ovt)


def _normalised_heads(acc_of_head):
    outs = []
    for h in range(NSA_HPG):
        acc = acc_of_head(h)
        lane = lax.broadcasted_iota(jnp.int32, acc.shape, 1)
        denom = jnp.sum(jnp.where(lane == NSA_DH, acc, 0.0), axis=-1, keepdims=True)
        outs.append(jnp.where(lane < NSA_DH, acc * (1.0 / denom), 0.0))
    return _merge_heads(outs).astype(BF16)


def _q_pair(q_ref, h, odd_group):
    x = q_ref[:, (h // 2) * LANES:(h // 2 + 1) * LANES].astype(F32) * (NSA_DH ** -0.5)
    odd_head = jnp.bool_(h % 2 == 1)
    x = jnp.where(odd_head != odd_group, pltpu.roll(x, NSA_DH, axis=1), x)
    lane_half = lax.broadcasted_iota(jnp.int32, x.shape, 1) // NSA_DH
    return jnp.where(lane_half == odd_group.astype(jnp.int32), x, 0.0).astype(BF16)


def _flash_sel_kernel(q_ref, b_ref, k_ref, oh_ref, v_ref, o_ref, qa_scr, m_scr, acc_scr, *, tq, tk, td):
    t0 = pl.program_id(1) * tq
    odd_group = (pl.program_id(0) % 2) == 1
    for h in range(NSA_HPG):
        qa_scr[h] = jnp.concatenate([_q_pair(q_ref, h, odd_group), b_ref[0]], axis=1)
    m_scr[...] = jnp.full(m_scr.shape, NEG, F32)
    acc_scr[...] = jnp.zeros(acc_scr.shape, F32)

    def tile(start, row0, row1, width, masked):
        k = jnp.concatenate([k_ref[pl.ds(start, width), :], oh_ref[pl.ds(start, width), :]], axis=1)
        v = v_ref[pl.ds(start, width), :]
        nr = row1 - row0
        if masked:
            t_q = t0 + row0 + lax.broadcasted_iota(jnp.int32, (nr, width), 0)
            ok = (start + lax.broadcasted_iota(jnp.int32, (nr, width), 1)) <= t_q
        scores = lambda h: _nt_dot(qa_scr[h, row0:row1, :], k)

        sc_next = scores(0)
        for h in range(NSA_HPG):
            sc = sc_next
            if h + 1 < NSA_HPG:
                sc_next = scores(h + 1)
            if masked:
                sc = jnp.where(ok, sc, NEG)
            m_prev = m_scr[h, row0:row1, :]
            m_new = jnp.maximum(m_prev, jnp.max(sc, axis=-1, keepdims=True))
            alpha = jnp.exp(m_prev - m_new)
            p = jnp.exp(sc - jnp.concatenate([m_new] * (width // LANES), axis=1))
            acc_scr[h, row0:row1, :] = alpha * acc_scr[h, row0:row1, :] + _dot(p.astype(BF16), v)
            m_scr[h, row0:row1, :] = m_new

    def body(kt, carry):
        tile(pl.multiple_of(kt * tk, tk), 0, tq, tk, False)
        return carry

    lax.fori_loop(0, t0 // tk, body, 0)
    for j in range(tq // td):
        tile(pl.multiple_of(t0 + j * td, td), j * td, tq, td, True)
    o_ref[...] = _normalised_heads(lambda h: acc_scr[h])


def _flash_win_kernel(q_ref, k_ref, v_ref, band_ref, o_ref, *, tq):
    t0 = pl.program_id(1) * tq
    odd_group = (pl.program_id(0) % 2) == 1
    n_tiles = WINDOW // tq + 1
    ks, vs, biases = [], [], []
    for j in range(n_tiles):
        first = t0 + (j - n_tiles + 1) * tq
        start = pl.multiple_of(jnp.maximum(first, 0), tq)
        ks.append(k_ref[pl.ds(start, tq), :])
        vs.append(v_ref[pl.ds(start, tq), :])
        biases.append(jnp.where(first < 0, NEG, band_ref[j]))

    def scores(h0):
        q = jnp.concatenate([_q_pair(q_ref, h0 + d, odd_group) for d in range(WIN_STACK)], axis=0)
        return [_nt_dot(q, ks[j]) + jnp.concatenate([biases[j]] * WIN_STACK, axis=0) for j in range(n_tiles)]

    accs = []
    scs_next = scores(0)
    for h0 in range(0, NSA_HPG, WIN_STACK):
        scs = scs_next
        if h0 + WIN_STACK < NSA_HPG:
            scs_next = scores(h0 + WIN_STACK)
        m = jnp.max(scs[0], axis=-1, keepdims=True)
        for sc in scs[1:]:
            m = jnp.maximum(m, jnp.max(sc, axis=-1, keepdims=True))
        acc = _dot(jnp.exp(scs[0] - m).astype(BF16), vs[0])
        for j in range(1, n_tiles):
            acc = acc + _dot(jnp.exp(scs[j] - m).astype(BF16), vs[j])
        accs.extend(acc[d * tq:(d + 1) * tq] for d in range(WIN_STACK))

    o_ref[...] = _normalised_heads(lambda h: accs[h])


def _flash(z, sel_bias, blk_onehot, b, s, tq, tk, td, mode):
    t = b * s
    nq = s // tq
    bg = b * NSA_KV_HEADS
    grp = lambda n: n % NSA_KV_HEADS
    bat = lambda n: n // NSA_KV_HEADS
    off_k, off_v = (OFF_KS, OFF_VS) if mode == "sel" else (OFF_KW, OFF_VW)
    q_spec = pl.BlockSpec((tq, 256), lambda n, i: (bat(n) * nq + i, OFF_NQ // 256 + grp(n)))
    k_spec = pl.BlockSpec((s, LANES), lambda n, i: (bat(n), off_k // LANES + grp(n) // 2))
    v_spec = pl.BlockSpec((s, LANES), lambda n, i: (bat(n), off_v // LANES + grp(n)))
    if mode == "sel":
        body = functools.partial(_flash_sel_kernel, tq=tq, tk=tk, td=td)
        in_specs = [q_spec, pl.BlockSpec((1, tq, LANES), lambda n, i: (n, i, 0)), k_spec,
                    pl.BlockSpec((s, LANES), lambda n, i: (0, 0)), v_spec]
        args = (z, sel_bias, z, blk_onehot, z)
        scratch = [pltpu.VMEM((NSA_HPG, tq, 2 * LANES), BF16), pltpu.VMEM((NSA_HPG, tq, LANES), F32),
                   pltpu.VMEM((NSA_HPG, tq, LANES), F32)]
    else:
        body = functools.partial(_flash_win_kernel, tq=tq)
        n_tiles = WINDOW // tq + 1
        rel = (np.arange(tq)[None, :, None] - np.arange(tq)[None, None, :]
               + (n_tiles - 1 - np.arange(n_tiles))[:, None, None] * tq)
        band = jnp.asarray(np.where((rel >= 0) & (rel < WINDOW), 0.0, NEG), F32)
        in_specs = [q_spec, k_spec, v_spec, pl.BlockSpec((n_tiles, tq, tq), lambda n, i: (0, 0, 0))]
        args = (z, z, z, band)
        scratch = []
    return pl.pallas_call(
        body,
        grid=(bg, nq),
        in_specs=in_specs,
        out_specs=pl.BlockSpec((tq, 256), lambda n, i: (bat(n) * nq + i, grp(n))),
        out_shape=jax.ShapeDtypeStruct((t, D_MODEL), BF16),
        scratch_shapes=scratch,
        compiler_params=_cparams(("arbitrary", "arbitrary")),
        name="nsa_flash_" + mode,
    )(*args)


def _shifted(u, prev, n):
    rolled = pltpu.roll(u, n, axis=0)
    head = rolled[:HALO_ROWS]
    row = lax.broadcasted_iota(jnp.int32, head.shape, 0)
    for r in range(n):
        head = jnp.where(row == r, prev[HALO_ROWS - n + r:HALO_ROWS - n + r + 1, :], head)
    return jnp.concatenate([head, rolled[HALO_ROWS:]], axis=0)


def _causal_conv3(u, prev, w):
    return w[0:1, :] * _shifted(u, prev, 2) + w[1:2, :] * _shifted(u, prev, 1) + w[2:3, :] * u


def _merge_kernel(x_ref, yret_ref, sb_ref, sc_ref, sx_ref, sch_ref, sxh_ref, mg0_ref, mg1_ref, mg2_ref,
                  ng_ref, ocmp_ref, oslc_ref, owin_ref, cw_ref, e_ref, wret_ref, wsc_ref, wnsa_ref,
                  wmix_ref, out_ref, *, tiles_per_seq):
    first = (pl.program_id(0) % tiles_per_seq) == 0
    u = sc_ref[...].astype(F32) * sx_ref[...].astype(F32)
    prev = sch_ref[...].astype(F32) * sxh_ref[...].astype(F32)
    prev = jnp.where(first, 0.0, prev)
    y_sc = (sb_ref[...].astype(F32) * _causal_conv3(u, prev, cw_ref[...])).astype(BF16)

    gates = _dot(_sigmoid(ng_ref[...].astype(F32)).astype(BF16), e_ref[...])
    y_nsa = (gates[:, 0:D_MODEL] * ocmp_ref[...].astype(F32)
             + gates[:, D_MODEL:2 * D_MODEL] * oslc_ref[...].astype(F32)
             + gates[:, 2 * D_MODEL:3 * D_MODEL] * owin_ref[...].astype(F32)).astype(BF16)

    merged = (_sigmoid(mg0_ref[...].astype(F32)) * _dot(yret_ref[...], wret_ref[...])
              + _sigmoid(mg1_ref[...].astype(F32)) * _dot(y_sc, wsc_ref[...])
              + _sigmoid(mg2_ref[...].astype(F32)) * _dot(y_nsa, wnsa_ref[...]))
    out_ref[...] = x_ref[...] + _dot(merged.astype(BF16), wmix_ref[...])


def _merge(x2, z, y_ret, o_cmp, o_slc, o_win, conv_w, gate_expand, w_ret, w_sc, w_nsa, w_mix, layer, s, tm):
    t = x2.shape[0]
    row = lambda c: (lambda i: (i, c))
    halo = lambda c: (lambda i: (jnp.maximum(i * (tm // HALO_ROWS) - 1, 0), c))
    const = lambda i: (0, 0)
    wspec = pl.BlockSpec((None, D_MODEL, D_MODEL), lambda i: (layer, 0, 0),
                         pipeline_mode=pl.Buffered(1))
    return pl.pallas_call(
        functools.partial(_merge_kernel, tiles_per_seq=s // tm),
        grid=(t // tm,),
        in_specs=[
            pl.BlockSpec((tm, D_MODEL), row(0)),
            pl.BlockSpec((tm, D_MODEL), row(0)),
            pl.BlockSpec((tm, D_MODEL), row(OFF_SB // D_MODEL)),
            pl.BlockSpec((tm, D_MODEL), row(OFF_SC // D_MODEL)),
            pl.BlockSpec((tm, D_MODEL), row(OFF_SX // D_MODEL)),
            pl.BlockSpec((HALO_ROWS, D_MODEL), halo(OFF_SC // D_MODEL)),
            pl.BlockSpec((HALO_ROWS, D_MODEL), halo(OFF_SX // D_MODEL)),
            pl.BlockSpec((tm, D_MODEL), row(OFF_MG // D_MODEL)),
            pl.BlockSpec((tm, D_MODEL), row(OFF_MG // D_MODEL + 1)),
            pl.BlockSpec((tm, D_MODEL), row(OFF_MG // D_MODEL + 2)),
            pl.BlockSpec((tm, LANES), row(OFF_NG // LANES)),
            pl.BlockSpec((tm, D_MODEL), row(0)),
            pl.BlockSpec((tm, D_MODEL), row(0)),
            pl.BlockSpec((tm, D_MODEL), row(0)),
            pl.BlockSpec((CONV_WIDTH, D_MODEL), const),
            pl.BlockSpec((LANES, 3 * D_MODEL), const),
            wspec, wspec, wspec, wspec,
        ],
        out_specs=pl.BlockSpec((tm, D_MODEL), row(0)),
        out_shape=jax.ShapeDtypeStruct((t, D_MODEL), F32),
        compiler_params=_cparams(("arbitrary",)),
        name="merge",
    )(x2, y_ret, z, z, z, z, z, z, z, z, z, o_cmp, o_slc, o_win, conv_w, gate_expand,
      w_ret, w_sc, w_nsa, w_mix)


def _ffn_kernel(x_ref, nw_ref, wup_ref, cw_ref, wd_ref, fw_ref, out_ref, h_scr, ca_scr, cv_scr,
                *, tiles_per_seq, final_norm, tf):
    @pl.when((pl.program_id(0) % tiles_per_seq) == 0)
    def _():
        ca_scr[...] = jnp.zeros_like(ca_scr)
        cv_scr[...] = jnp.zeros_like(cv_scr)

    h_scr[...] = _rms(x_ref[...], nw_ref[...]).astype(BF16)
    out_ref[...] = x_ref[...]
    tm = h_scr.shape[0]
    nf = D_FF // tf
    cols_a = lambda j: slice(j * tf, (j + 1) * tf)
    cols_v = lambda j: slice(D_FF + j * tf, D_FF + (j + 1) * tf)

    def up(j):
        h = h_scr[...]
        return _dot(h, wup_ref[:, cols_a(j)]), _dot(h, wup_ref[:, cols_v(j)])

    def gate(j, ua, uv):
        a = _causal_conv3(ua, ca_scr[j], cw_ref[:, cols_a(j)])
        v = _causal_conv3(uv, cv_scr[j], cw_ref[:, cols_v(j)])
        ca_scr[j] = ua[tm - HALO_ROWS:, :]
        cv_scr[j] = uv[tm - HALO_ROWS:, :]
        return (a * _sigmoid(a) * v).astype(BF16)

    u_next = up(0)
    act_prev = None
    for j in range(nf):
        u_cur = u_next
        if j + 1 < nf:
            u_next = up(j + 1)
        act = gate(j, *u_cur)
        if act_prev is not None:
            out_ref[...] += _dot(act_prev, wd_ref[cols_a(j - 1), :])
        act_prev = act
    out_ref[...] += _dot(act_prev, wd_ref[cols_a(nf - 1), :])
    if final_norm:
        out_ref[...] = _rms(out_ref[...], fw_ref[...])


def _ffn(x2, norm_w, w_up, conv_w, w_down, final_w, layer, s, tm, tf, final_norm):
    t = x2.shape[0]
    nf = D_FF // tf
    const = lambda i: (0, 0)
    of_layer = lambda i: (layer, 0, 0)
    resident = dict(index_map=of_layer, pipeline_mode=pl.Buffered(1))
    return pl.pallas_call(
        functools.partial(_ffn_kernel, tiles_per_seq=s // tm, final_norm=final_norm, tf=tf),
        grid=(t // tm,),
        in_specs=[
            pl.BlockSpec((tm, D_MODEL), lambda i: (i, 0)),
            pl.BlockSpec((1, D_MODEL), const),
            pl.BlockSpec((None, D_MODEL, 2 * D_FF), **resident),
            pl.BlockSpec((None, CONV_WIDTH, 2 * D_FF), of_layer),
            pl.BlockSpec((None, D_FF, D_MODEL), **resident),
            pl.BlockSpec((1, D_MODEL), const),
        ],
        out_specs=pl.BlockSpec((tm, D_MODEL), lambda i: (i, 0)),
        out_shape=jax.ShapeDtypeStruct((t, D_MODEL), F32),
        scratch_shapes=[
            pltpu.VMEM((tm, D_MODEL), BF16),
            pltpu.VMEM((nf, HALO_ROWS, tf), F32),
            pltpu.VMEM((nf, HALO_ROWS, tf), F32),
        ],
        compiler_params=_cparams(("arbitrary",)),
        name="ffn",
    )(x2, norm_w, w_up, conv_w, w_down, final_w)


def _prep_w_in(w_in_l):
    w_in_l = w_in_l.astype(BF16)
    gw = NSA_KV_HEADS * NSA_DH
    src = OFF_MG
    k_c, v_c, k_s, v_s, k_w, v_w = [w_in_l[:, src + n * gw: src + (n + 1) * gw] for n in range(6)]
    ng = w_in_l[:, src + 6 * gw: src + 6 * gw + 3 * NSA_HEADS]
    ng = ng.reshape(D_MODEL, NSA_HEADS, 3).transpose(0, 2, 1).reshape(D_MODEL, 3 * NSA_HEADS)
    mg = w_in_l[:, src + 6 * gw + 3 * NSA_HEADS:]

    def lane_pad(v):
        v = v.reshape(D_MODEL, NSA_KV_HEADS, NSA_DH)
        return jnp.pad(v, ((0, 0), (0, 0), (0, LANES - NSA_DH))).reshape(D_MODEL, NSA_KV_HEADS * LANES)

    pad = jnp.zeros((D_MODEL, Z_WIDTH - OFF_NG - 3 * NSA_HEADS), w_in_l.dtype)
    return jnp.concatenate([w_in_l[:, :OFF_MG], mg, k_c, v_c, k_s, k_w, lane_pad(v_s), lane_pad(v_w), ng, pad],
                           axis=1).astype(BF16)


def _z_bias():
    bias = np.zeros((1, Z_WIDTH), np.float32)
    for off in (OFF_VS, OFF_VW):
        for g in range(NSA_KV_HEADS):
            bias[0, off + g * LANES + NSA_DH] = 1.0
    return jnp.asarray(bias)


def _gate_expander():
    e = np.zeros((LANES, 3 * D_MODEL), np.float32)
    for br in range(3):
        for h in range(NSA_HEADS):
            e[br * NSA_HEADS + h, br * D_MODEL + h * NSA_DH: br * D_MODEL + (h + 1) * NSA_DH] = 1.0
    return jnp.asarray(e, BF16)


def _overlap_t(s, ncp):
    nc = (s - CMP_BLOCK) // CMP_STRIDE + 1
    ns = s // SLC_BLOCK
    ci = np.arange(ncp) * CMP_STRIDE
    sj = np.arange(ns) * SLC_BLOCK
    ov = ((ci[None, :] < sj[:, None] + SLC_BLOCK) & (ci[None, :] + CMP_BLOCK > sj[:, None])
          & (np.arange(ncp)[None, :] < nc))
    return jnp.asarray(ov.astype(np.float32), BF16)


def kernel(x, attn_norm_w, w_in, ret_norm_w, w_ret_out, sc_conv_w, w_sc_out, nsa_cmp_pos, nsa_cmp_w1,
           nsa_cmp_w2, w_nsa_out, w_mix_out, ffn_norm_w, w_ffn_up, ffn_conv_w, w_ffn_down, final_norm_w):
    b, s, _ = x.shape
    t = b * s
    depth = w_in.shape[0]
    bg = b * NSA_KV_HEADS
    ncp = s // CMP_STRIDE
    assert s % 256 == 0 and s // SLC_BLOCK <= NSA_DH

    tm_in = min(1024, t)
    tm_merge = 512
    tm_ffn = min(1024, s)
    tq_cmp = 512
    tq_sel = 1024
    tk_sel = 512
    td_sel = 512
    tq_win = 256
    tr = 256

    ret_consts = _retention_consts(s)
    gate_expand = _gate_expander()
    z_bias = _z_bias()
    ovt = _overlap_t(s, ncp)
    onehot = np.zeros((s, LANES), np.float32)
    onehot[np.arange(s), (np.arange(s) // SLC_BLOCK) % NSA_DH] = 1.0
    blk_onehot = jnp.asarray(onehot, BF16)
    w_ret_bf, w_sc_bf, w_nsa_bf, w_mix_bf, w_up_bf, w_down_bf = (
        w.astype(BF16) for w in (w_ret_out, w_sc_out, w_nsa_out, w_mix_out, w_ffn_up, w_ffn_down))

    x2 = x.reshape(t, D_MODEL)
    for l in range(depth):
        z = _inproj(x2, attn_norm_w[l][None], _prep_w_in(w_in[l]), z_bias, tm_in, Z_TILE)
        y_ret = _retention(z, ret_norm_w[l][None], ret_consts, b, s, tr)

        kv_c = z[:, OFF_KC:OFF_KC + 2 * NSA_KV_HEADS * NSA_DH]
        kv_c = kv_c.reshape(b, ncp, CMP_STRIDE, 2, NSA_KV_HEADS, NSA_DH).transpose(3, 0, 4, 1, 2, 5)
        kv_c = kv_c.reshape(2, bg, ncp, CMP_STRIDE * NSA_DH)
        pos_flat = nsa_cmp_pos[l].reshape(2, 1, CMP_BLOCK * NSA_DH)
        w2p = jnp.pad(nsa_cmp_w2[l], ((0, 0), (0, 0), (0, LANES - NSA_DH))).astype(BF16)
        kvc = _compress(kv_c, pos_flat, nsa_cmp_w1[l].astype(BF16), w2p)

        o_cmp, sel_bias = _cmp_select(z, kvc, ovt, b, s, tq_cmp)
        o_slc = _flash(z, sel_bias, blk_onehot, b, s, tq_sel, tk_sel, td_sel, "sel")
        o_win = _flash(z, None, None, b, s, tq_win, None, None, "win")

        x2 = _merge(x2, z, y_ret, o_cmp, o_slc, o_win, sc_conv_w[l], gate_expand,
                    w_ret_bf, w_sc_bf, w_nsa_bf, w_mix_bf, l, s, tm_merge)
        x2 = _ffn(x2, ffn_norm_w[l][None], w_up_bf, ffn_conv_w, w_down_bf, final_norm_w[None],
                  l, s, tm_ffn, 256, l == depth - 1)
    return x2.reshape(b, s, D_MODEL)
```

```python
import functools

import numpy as np
import jax
import jax.numpy as jnp
from jax import lax
from jax.experimental import pallas as pl
from jax.experimental.pallas import tpu as pltpu

F32 = jnp.float32
BF16 = jnp.bfloat16

D_MODEL = 1024
RET_HEADS = 4
RET_DV = D_MODEL // RET_HEADS
RET_DK = RET_DV // 2
RET_CHUNK = 128
CONV_WIDTH = 3
NSA_DH = 64
NSA_HEADS = D_MODEL // NSA_DH
NSA_KV_HEADS = 4
NSA_HPG = NSA_HEADS // NSA_KV_HEADS
CMP_BLOCK = 32
CMP_STRIDE = 16
CMP_HIDDEN = 256
SLC_BLOCK = 64
N_SELECT = 16
WINDOW = 512
D_FF = ((8 * D_MODEL // 3 + 127) // 128) * 128
EPS = 1e-6
NEG = -1e30
FORCE = 1e6

WIN_STACK = 2
LANES = 128
HALO_ROWS = 8

OFF_RQ = 0
OFF_RK = 512
OFF_RV = 1024
OFF_RG = 2048
OFF_SB = 3072
OFF_SC = 4096
OFF_SX = 5120
OFF_NQ = 6144
OFF_MG = 7168
OFF_KC = 10240
OFF_VC = 10496
OFF_KS = 10752
OFF_KW = 11008
OFF_VS = 11264
OFF_VW = 11776
OFF_NG = 12288
Z_WIDTH = 12800
Z_TILE = 2560

VMEM_LIMIT = 56 * 1024 * 1024


def _cparams(sem):
    return pltpu.CompilerParams(dimension_semantics=sem, vmem_limit_bytes=VMEM_LIMIT)


def _nt_dot(a, b):
    return lax.dot_general(a, b, (((1,), (1,)), ((), ())), preferred_element_type=F32)


def _dot(a, b):
    return jnp.dot(a, b, preferred_element_type=F32)


def _sigmoid(x):
    return 1.0 / (1.0 + jnp.exp(-x))


def _rms(x, w):
    return x * lax.rsqrt(jnp.mean(x * x, axis=-1, keepdims=True) + EPS) * w


def _inproj_kernel(x_ref, nw_ref, w_ref, b_ref, z_ref, h_scr):
    @pl.when(pl.program_id(1) == 0)
    def _():
        h_scr[...] = _rms(x_ref[...], nw_ref[...]).astype(BF16)

    z_ref[...] = (_dot(h_scr[...], w_ref[...]) + b_ref[...]).astype(BF16)


def _inproj(x2, norm_w, w_in_p, z_bias, tm, tn):
    t = x2.shape[0]
    return pl.pallas_call(
        _inproj_kernel,
        grid=(t // tm, Z_WIDTH // tn),
        in_specs=[
            pl.BlockSpec((tm, D_MODEL), lambda i, j: (i, 0)),
            pl.BlockSpec((1, D_MODEL), lambda i, j: (0, 0)),
            pl.BlockSpec((D_MODEL, tn), lambda i, j: (0, j)),
            pl.BlockSpec((1, tn), lambda i, j: (0, j)),
        ],
        out_specs=pl.BlockSpec((tm, tn), lambda i, j: (i, j)),
        out_shape=jax.ShapeDtypeStruct((t, Z_WIDTH), BF16),
        scratch_shapes=[pltpu.VMEM((tm, D_MODEL), BF16)],
        compiler_params=_cparams(("arbitrary", "arbitrary")),
        name="inproj",
    )(x2, norm_w, w_in_p, z_bias)


def _ret_kernel(q_ref, k_ref, v_ref, g_ref, cos_ref, sin_ref, dm_ref, zeta_ref, xi_ref,
                dec_ref, nw_ref, y_ref, st_scr, *, n_chunks):
    @pl.when(pl.program_id(1) == 0)
    def _():
        st_scr[...] = jnp.zeros_like(st_scr)

    c = RET_CHUNK
    for ci in range(n_chunks):
        rows = slice(ci * c, (ci + 1) * c)
        cos = cos_ref[rows, :]
        sin = sin_ref[rows, :]
        for h in range(RET_HEADS):
            q = q_ref[rows, h * RET_DK:(h + 1) * RET_DK].astype(F32)
            k = k_ref[rows, h * RET_DK:(h + 1) * RET_DK].astype(F32)
            qr = q * cos + pltpu.roll(q, RET_DK // 2, axis=1) * sin
            kr = (k * cos + pltpu.roll(k, RET_DK // 2, axis=1) * sin) * (RET_DK ** -0.5)
            qb = qr.astype(BF16)
            kb = kr.astype(BF16)
            v = v_ref[rows, h * RET_DV:(h + 1) * RET_DV]
            scores = _nt_dot(qb, kb) * dm_ref[h]
            o = _dot(scores.astype(BF16), v)
            st = st_scr[h]
            o = o + _dot(qb, st.astype(BF16)) * xi_ref[h]
            vz = (v.astype(F32) * zeta_ref[h]).astype(BF16)
            kv = _dot(kr.T.astype(BF16), vz)
            st_scr[h] = st * dec_ref[h] + kv
            mu = jnp.mean(o, axis=-1, keepdims=True)
            d = o - mu
            var = jnp.mean(d * d, axis=-1, keepdims=True)
            on = d * lax.rsqrt(var + EPS) * nw_ref[:, h * RET_DV:(h + 1) * RET_DV]
            g = g_ref[rows, h * RET_DV:(h + 1) * RET_DV].astype(F32)
            y_ref[rows, h * RET_DV:(h + 1) * RET_DV] = (g * _sigmoid(g) * on).astype(BF16)


def _retention_consts(s):
    c = RET_CHUNK
    pos = jnp.arange(s, dtype=F32)
    theta = 10000.0 ** (-jnp.linspace(0.0, 1.0, RET_DK // 2, dtype=F32))
    ang = pos[:, None] * theta[None, :]
    cos = jnp.cos(ang)
    sin = jnp.sin(ang)
    cos2 = jnp.concatenate([cos, cos], axis=-1)
    sin2 = jnp.concatenate([-sin, sin], axis=-1)
    log_gamma = jnp.log1p(-(2.0 ** (-5.0 - jnp.arange(RET_HEADS, dtype=F32))))
    j = jnp.arange(c, dtype=F32)
    rel = j[:, None] - j[None, :]
    dmask = jnp.where(rel >= 0, jnp.exp(log_gamma[:, None, None] * jnp.maximum(rel, 0.0)), 0.0)
    zeta = jnp.exp(log_gamma[:, None] * (c - 1 - j)[None, :])
    xi = jnp.exp(log_gamma[:, None] * (j + 1.0)[None, :])
    dec = jnp.exp(log_gamma * c)
    zeta_b = jnp.broadcast_to(zeta[:, :, None], (RET_HEADS, c, RET_DV))
    xi_b = jnp.broadcast_to(xi[:, :, None], (RET_HEADS, c, RET_DV))
    dec_b = jnp.broadcast_to(dec[:, None, None], (RET_HEADS, RET_DK, RET_DV))
    return cos2, sin2, dmask, zeta_b, xi_b, dec_b


def _retention(z, ret_norm_w, consts, b, s, tr):
    t = b * s
    nt = s // tr
    cos2, sin2, dmask, zeta_b, xi_b, dec_b = consts
    const3 = lambda bi, i: (0, 0, 0)
    return pl.pallas_call(
        functools.partial(_ret_kernel, n_chunks=tr // RET_CHUNK),
        grid=(b, nt),
        in_specs=[
            pl.BlockSpec((tr, 512), lambda bi, i: (bi * nt + i, OFF_RQ // 512)),
            pl.BlockSpec((tr, 512), lambda bi, i: (bi * nt + i, OFF_RK // 512)),
            pl.BlockSpec((tr, 1024), lambda bi, i: (bi * nt + i, OFF_RV // 1024)),
            pl.BlockSpec((tr, 1024), lambda bi, i: (bi * nt + i, OFF_RG // 1024)),
            pl.BlockSpec((tr, RET_DK), lambda bi, i: (i, 0)),
            pl.BlockSpec((tr, RET_DK), lambda bi, i: (i, 0)),
            pl.BlockSpec((RET_HEADS, RET_CHUNK, RET_CHUNK), const3),
            pl.BlockSpec((RET_HEADS, RET_CHUNK, RET_DV), const3),
            pl.BlockSpec((RET_HEADS, RET_CHUNK, RET_DV), const3),
            pl.BlockSpec((RET_HEADS, RET_DK, RET_DV), const3),
            pl.BlockSpec((1, D_MODEL), lambda bi, i: (0, 0)),
        ],
        out_specs=pl.BlockSpec((tr, D_MODEL), lambda bi, i: (bi * nt + i, 0)),
        out_shape=jax.ShapeDtypeStruct((t, D_MODEL), BF16),
        scratch_shapes=[pltpu.VMEM((RET_HEADS, RET_DK, RET_DV), F32)],
        compiler_params=_cparams(("arbitrary", "arbitrary")),
        name="retention",
    )(z, z, z, z, cos2, sin2, dmask, zeta_b, xi_b, dec_b, ret_norm_w)


def _compress_kernel(rows_ref, pos_ref, w1_ref, w2_ref, out_ref):
    half = rows_ref.shape[-1]
    x = rows_ref[0, 0].astype(F32)
    lo = _dot((x + pos_ref[0, :, :half]).astype(BF16), w1_ref[0, :half, :])
    hi = _dot((x + pos_ref[0, :, half:]).astype(BF16), w1_ref[0, half:, :])
    hid = lo + pltpu.roll(hi, hi.shape[0] - 1, axis=0)
    gel = 0.5 * hid * (1.0 + jnp.tanh(np.sqrt(2.0 / np.pi) * (hid + 0.044715 * (hid * hid * hid))))
    out_ref[0, 0] = _dot(gel.astype(BF16), w2_ref[0]).astype(BF16)


def _compress(rows, pos_flat, w1, w2p):
    _, bg, ncp, half = rows.shape
    width = 2 * half
    return pl.pallas_call(
        _compress_kernel,
        grid=(2, bg),
        in_specs=[
            pl.BlockSpec((1, 1, ncp, half), lambda a, n: (a, n, 0, 0)),
            pl.BlockSpec((1, 1, width), lambda a, n: (a, 0, 0)),
            pl.BlockSpec((1, width, CMP_HIDDEN), lambda a, n: (a, 0, 0)),
            pl.BlockSpec((1, CMP_HIDDEN, LANES), lambda a, n: (a, 0, 0)),
        ],
        out_specs=pl.BlockSpec((1, 1, ncp, LANES), lambda a, n: (a, n, 0, 0)),
        out_shape=jax.ShapeDtypeStruct((2, bg, ncp, LANES), BF16),
        compiler_params=_cparams(("arbitrary", "arbitrary")),
        name="nsa_compress",
    )(rows, pos_flat, w1, w2p)


def _head_pad(q2_ref_block, h):
    x = q2_ref_block[:, (h // 2) * LANES:(h // 2 + 1) * LANES].astype(F32)
    if h % 2:
        x = pltpu.roll(x, NSA_DH, axis=1)
    lane = lax.broadcasted_iota(jnp.int32, x.shape, 1)
    return jnp.where(lane < NSA_DH, x, 0.0)


def _merge_heads(parts):
    lo = parts[0] + pltpu.roll(parts[1], NSA_DH, axis=1)
    hi = parts[2] + pltpu.roll(parts[3], NSA_DH, axis=1)
    return jnp.concatenate([lo, hi], axis=1)


def _cmp_select_kernel(q_ref, kc_ref, vc_ref, ovt_ref, ocmp_ref, bias_ref, *, tq, n_blk):
    for ti in range(q_ref.shape[0] // tq):
        _cmp_select_tile(ti * tq, q_ref, kc_ref, vc_ref, ovt_ref, ocmp_ref, bias_ref, tq, n_blk)


def _cmp_select_tile(t0, q_ref, kc_ref, vc_ref, ovt_ref, ocmp_ref, bias_ref, tq, n_blk_all):
    rows = slice(t0, t0 + tq)
    ncp = min(kc_ref.shape[2], -(-((t0 + tq) // CMP_STRIDE) // LANES) * LANES)
    n_blk = min(n_blk_all, -(-((t0 + tq) // SLC_BLOCK) // 8) * 8)
    kc = kc_ref[0, 0, :ncp, :]
    vc = vc_ref[0, 0, :ncp, :]
    ovt = ovt_ref[:n_blk, :ncp]
    scale = NSA_DH ** -0.5

    t_q = t0 + lax.broadcasted_iota(jnp.int32, (tq, ncp), 0)
    c_id = lax.broadcasted_iota(jnp.int32, (tq, ncp), 1)
    valid = (c_id * CMP_STRIDE + (CMP_BLOCK - 1)) <= t_q
    t_col = t0 + lax.broadcasted_iota(jnp.int32, (tq, 1), 0)
    any1 = jnp.where(t_col >= CMP_BLOCK - 1, 1.0, 0.0)

    imp = jnp.zeros((n_blk, tq), F32)
    o_parts = []
    for h in range(NSA_HPG):
        qp = (_head_pad(q_ref.at[rows, :], h) * scale).astype(BF16)
        s1 = jnp.where(valid, _nt_dot(qp, kc), NEG)
        m = jnp.max(s1, axis=-1, keepdims=True)
        e = jnp.exp(s1 - m)
        pb = (e * ((1.0 / jnp.sum(e, axis=-1, keepdims=True)) * any1)).astype(BF16)
        o_parts.append(_dot(pb, vc))
        imp = imp + _nt_dot(ovt, pb)
    ocmp_ref[rows, :] = _merge_heads(o_parts).astype(BF16)

    blk = lax.broadcasted_iota(jnp.int32, (n_blk, tq), 0)
    t_row = t0 + lax.broadcasted_iota(jnp.int32, (n_blk, tq), 1)
    cur = t_row // SLC_BLOCK
    causal = blk <= cur
    forced = (blk == 0) | (blk == cur) | (blk == cur - 1)
    imp = jnp.where(forced, FORCE, imp)
    imp = jnp.where(causal, imp, NEG)
    sub = 8
    groups = [imp[g * sub:(g + 1) * sub, :] for g in range(n_blk // sub)]
    ranks = [jnp.zeros((sub, tq), F32) for _ in groups]
    row_in_group = lax.broadcasted_iota(jnp.int32, (sub, tq), 0)
    for i in range(n_blk):
        r = jnp.broadcast_to(imp[i:i + 1, :], (sub, tq))
        for g, x in enumerate(groups):
            if g < i // sub:
                ahead = jnp.where(r > x, 1.0, 0.0)
            elif g > i // sub:
                ahead = jnp.where(r >= x, 1.0, 0.0)
            else:
                ahead = jnp.where(row_in_group > i % sub, jnp.where(r >= x, 1.0, 0.0), jnp.where(r > x, 1.0, 0.0))
            ranks[g] = ranks[g] + ahead
    rank = jnp.concatenate(ranks, axis=0)
    bias_t = jnp.where(rank < float(min(N_SELECT, n_blk_all)), 0.0, NEG)
    pieces = [bias_t]
    if n_blk < NSA_DH:
        pieces.append(jnp.full((NSA_DH - n_blk, tq), NEG, F32))
    pieces.append(jnp.zeros((LANES - NSA_DH, tq), F32))
    bias_ref[0, rows, :] = jnp.concatenate(pieces, axis=0).T.astype(BF16)


def _cmp_select(z, kvc, ovt, b, s, tq):
    t = b * s
    bg = b * NSA_KV_HEADS
    ncp = kvc.shape[2]
    n_blk = s // SLC_BLOCK
    qcol = OFF_NQ // 256
    return pl.pallas_call(
        functools.partial(_cmp_select_kernel, tq=tq, n_blk=n_blk),
        grid=(bg,),
        in_specs=[
            pl.BlockSpec((s, 256), lambda n: (n // NSA_KV_HEADS, qcol + n % NSA_KV_HEADS)),
            pl.BlockSpec((1, 1, ncp, LANES), lambda n: (0, n, 0, 0)),
            pl.BlockSpec((1, 1, ncp, LANES), lambda n: (1, n, 0, 0)),
            pl.BlockSpec((n_blk, ncp), lambda n: (0, 0)),
        ],
        out_specs=[
            pl.BlockSpec((s, 256), lambda n: (n // NSA_KV_HEADS, n % NSA_KV_HEADS)),
            pl.BlockSpec((1, s, LANES), lambda n: (n, 0, 0)),
        ],
        out_shape=[
            jax.ShapeDtypeStruct((t, D_MODEL), BF16),
            jax.ShapeDtypeStruct((bg, s, LANES), BF16),
        ],
        compiler_params=_cparams(("arbitrary",)),
        name="nsa_cmp_select",
    )(z, kvc, kvc, ovt)


def _normalised_heads(acc_of_head):
    outs = []
    for h in range(NSA_HPG):
        acc = acc_of_head(h)
        lane = lax.broadcasted_iota(jnp.int32, acc.shape, 1)
        denom = jnp.sum(jnp.where(lane == NSA_DH, acc, 0.0), axis=-1, keepdims=True)
        outs.append(jnp.where(lane < NSA_DH, acc * (1.0 / denom), 0.0))
    return _merge_heads(outs).astype(BF16)


def _q_pair(q_ref, h, odd_group):
    x = q_ref[:, (h // 2) * LANES:(h // 2 + 1) * LANES].astype(F32) * (NSA_DH ** -0.5)
    odd_head = jnp.bool_(h % 2 == 1)
    x = jnp.where(odd_head != odd_group, pltpu.roll(x, NSA_DH, axis=1), x)
    lane_half = lax.broadcasted_iota(jnp.int32, x.shape, 1) // NSA_DH
    return jnp.where(lane_half == odd_group.astype(jnp.int32), x, 0.0).astype(BF16)


def _flash_sel_kernel(q_ref, b_ref, k_ref, oh_ref, v_ref, o_ref, qa_scr, m_scr, acc_scr, *, tq, tk, td):
    t0 = pl.program_id(1) * tq
    odd_group = (pl.program_id(0) % 2) == 1
    for h in range(NSA_HPG):
        qa_scr[h] = jnp.concatenate([_q_pair(q_ref, h, odd_group), b_ref[0]], axis=1)
    m_scr[...] = jnp.full(m_scr.shape, NEG, F32)
    acc_scr[...] = jnp.zeros(acc_scr.shape, F32)

    def tile(start, row0, row1, width, masked):
        k = jnp.concatenate([k_ref[pl.ds(start, width), :], oh_ref[pl.ds(start, width), :]], axis=1)
        v = v_ref[pl.ds(start, width), :]
        nr = row1 - row0
        if masked:
            ok = (lax.broadcasted_iota(jnp.int32, (nr, width), 1)
                  <= lax.broadcasted_iota(jnp.int32, (nr, width), 0))
        scores = lambda h: _nt_dot(qa_scr[h, row0:row1, :], k)

        sc_next = scores(0)
        for h in range(NSA_HPG):
            sc = sc_next
            if h + 1 < NSA_HPG:
                sc_next = scores(h + 1)
            if masked:
                sc = jnp.where(ok, sc, NEG)
            m_prev = m_scr[h, row0:row1, :]
            m_new = jnp.maximum(m_prev, jnp.max(sc, axis=-1, keepdims=True))
            alpha = jnp.exp(m_prev - m_new)
            p = jnp.exp(sc - jnp.concatenate([m_new] * (width // LANES), axis=1))
            acc_scr[h, row0:row1, :] = alpha * acc_scr[h, row0:row1, :] + _dot(p.astype(BF16), v)
            m_scr[h, row0:row1, :] = m_new

    def body(kt, carry):
        tile(pl.multiple_of(kt * tk, tk), 0, tq, tk, False)
        return carry

    lax.fori_loop(0, t0 // tk, body, 0)
    for j in range(tq // td):
        tile(pl.multiple_of(t0 + j * td, td), j * td, tq, td, True)
    o_ref[...] = _normalised_heads(lambda h: acc_scr[h])


def _flash_win_kernel(q_ref, k_ref, v_ref, band_ref, o_ref, *, tq):
    t0 = pl.program_id(1) * tq
    odd_group = (pl.program_id(0) % 2) == 1
    n_tiles = WINDOW // tq + 1
    ks, vs, biases = [], [], []
    for j in range(n_tiles):
        first = t0 + (j - n_tiles + 1) * tq
        start = pl.multiple_of(jnp.maximum(first, 0), tq)
        ks.append(k_ref[pl.ds(start, tq), :])
        vs.append(v_ref[pl.ds(start, tq), :])
        biases.append(jnp.where(first < 0, NEG, band_ref[j]))

    def scores(h0):
        q = jnp.concatenate([_q_pair(q_ref, h0 + d, odd_group) for d in range(WIN_STACK)], axis=0)
        return [_nt_dot(q, ks[j]) + jnp.concatenate([biases[j]] * WIN_STACK, axis=0) for j in range(n_tiles)]

    accs = []
    scs_next = scores(0)
    for h0 in range(0, NSA_HPG, WIN_STACK):
        scs = scs_next
        if h0 + WIN_STACK < NSA_HPG:
            scs_next = scores(h0 + WIN_STACK)
        m = jnp.max(scs[0], axis=-1, keepdims=True)
        for sc in scs[1:]:
            m = jnp.maximum(m, jnp.max(sc, axis=-1, keepdims=True))
        acc = _dot(jnp.exp(scs[0] - m).astype(BF16), vs[0])
        for j in range(1, n_tiles):
            acc = acc + _dot(jnp.exp(scs[j] - m).astype(BF16), vs[j])
        accs.extend(acc[d * tq:(d + 1) * tq] for d in range(WIN_STACK))

    o_ref[...] = _normalised_heads(lambda h: accs[h])


def _flash(z, sel_bias, blk_onehot, b, s, tq, tk, td, mode):
    t = b * s
    nq = s // tq
    bg = b * NSA_KV_HEADS
    grp = lambda n: n % NSA_KV_HEADS
    bat = lambda n: n // NSA_KV_HEADS
    off_k, off_v = (OFF_KS, OFF_VS) if mode == "sel" else (OFF_KW, OFF_VW)
    q_spec = pl.BlockSpec((tq, 256), lambda n, i: (bat(n) * nq + i, OFF_NQ // 256 + grp(n)))
    k_spec = pl.BlockSpec((s, LANES), lambda n, i: (bat(n), off_k // LANES + grp(n) // 2))
    v_spec = pl.BlockSpec((s, LANES), lambda n, i: (bat(n), off_v // LANES + grp(n)))
    if mode == "sel":
        body = functools.partial(_flash_sel_kernel, tq=tq, tk=tk, td=td)
        in_specs = [q_spec, pl.BlockSpec((1, tq, LANES), lambda n, i: (n, i, 0)), k_spec,
                    pl.BlockSpec((s, LANES), lambda n, i: (0, 0)), v_spec]
        args = (z, sel_bias, z, blk_onehot, z)
        scratch = [pltpu.VMEM((NSA_HPG, tq, 2 * LANES), BF16), pltpu.VMEM((NSA_HPG, tq, LANES), F32),
                   pltpu.VMEM((NSA_HPG, tq, LANES), F32)]
    else:
        body = functools.partial(_flash_win_kernel, tq=tq)
        n_tiles = WINDOW // tq + 1
        rel = (np.arange(tq)[None, :, None] - np.arange(tq)[None, None, :]
               + (n_tiles - 1 - np.arange(n_tiles))[:, None, None] * tq)
        band = jnp.asarray(np.where((rel >= 0) & (rel < WINDOW), 0.0, NEG), F32)
        in_specs = [q_spec, k_spec, v_spec, pl.BlockSpec((n_tiles, tq, tq), lambda n, i: (0, 0, 0))]
        args = (z, z, z, band)
        scratch = []
    return pl.pallas_call(
        body,
        grid=(bg, nq),
        in_specs=in_specs,
        out_specs=pl.BlockSpec((tq, 256), lambda n, i: (bat(n) * nq + i, grp(n))),
        out_shape=jax.ShapeDtypeStruct((t, D_MODEL), BF16),
        scratch_shapes=scratch,
        compiler_params=_cparams(("arbitrary", "arbitrary")),
        name="nsa_flash_" + mode,
    )(*args)


def _shifted(u, prev, n):
    rolled = pltpu.roll(u, n, axis=0)
    head = rolled[:HALO_ROWS]
    row = lax.broadcasted_iota(jnp.int32, head.shape, 0)
    for r in range(n):
        head = jnp.where(row == r, prev[HALO_ROWS - n + r:HALO_ROWS - n + r + 1, :], head)
    return jnp.concatenate([head, rolled[HALO_ROWS:]], axis=0)


def _causal_conv3(u, prev, w):
    return w[0:1, :] * _shifted(u, prev, 2) + w[1:2, :] * _shifted(u, prev, 1) + w[2:3, :] * u


def _merge_kernel(x_ref, yret_ref, sb_ref, sc_ref, sx_ref, sch_ref, sxh_ref, mg0_ref, mg1_ref, mg2_ref,
                  ng_ref, ocmp_ref, oslc_ref, owin_ref, cw_ref, e_ref, wret_ref, wsc_ref, wnsa_ref,
                  wmix_ref, out_ref, *, tiles_per_seq):
    first = (pl.program_id(0) % tiles_per_seq) == 0
    u = sc_ref[...].astype(F32) * sx_ref[...].astype(F32)
    prev = sch_ref[...].astype(F32) * sxh_ref[...].astype(F32)
    prev = jnp.where(first, 0.0, prev)
    y_sc = (sb_ref[...].astype(F32) * _causal_conv3(u, prev, cw_ref[...])).astype(BF16)

    gates = _dot(_sigmoid(ng_ref[...].astype(F32)).astype(BF16), e_ref[...])
    y_nsa = (gates[:, 0:D_MODEL] * ocmp_ref[...].astype(F32)
             + gates[:, D_MODEL:2 * D_MODEL] * oslc_ref[...].astype(F32)
             + gates[:, 2 * D_MODEL:3 * D_MODEL] * owin_ref[...].astype(F32)).astype(BF16)

    merged = (_sigmoid(mg0_ref[...].astype(F32)) * _dot(yret_ref[...], wret_ref[...])
              + _sigmoid(mg1_ref[...].astype(F32)) * _dot(y_sc, wsc_ref[...])
              + _sigmoid(mg2_ref[...].astype(F32)) * _dot(y_nsa, wnsa_ref[...]))
    out_ref[...] = x_ref[...] + _dot(merged.astype(BF16), wmix_ref[...])


def _merge(x2, z, y_ret, o_cmp, o_slc, o_win, conv_w, gate_expand, w_ret, w_sc, w_nsa, w_mix, layer, s, tm):
    t = x2.shape[0]
    row = lambda c: (lambda i: (i, c))
    halo = lambda c: (lambda i: (jnp.maximum(i * (tm // HALO_ROWS) - 1, 0), c))
    const = lambda i: (0, 0)
    wspec = pl.BlockSpec((None, D_MODEL, D_MODEL), lambda i: (layer, 0, 0),
                         pipeline_mode=pl.Buffered(1))
    return pl.pallas_call(
        functools.partial(_merge_kernel, tiles_per_seq=s // tm),
        grid=(t // tm,),
        in_specs=[
            pl.BlockSpec((tm, D_MODEL), row(0)),
            pl.BlockSpec((tm, D_MODEL), row(0)),
            pl.BlockSpec((tm, D_MODEL), row(OFF_SB // D_MODEL)),
            pl.BlockSpec((tm, D_MODEL), row(OFF_SC // D_MODEL)),
            pl.BlockSpec((tm, D_MODEL), row(OFF_SX // D_MODEL)),
            pl.BlockSpec((HALO_ROWS, D_MODEL), halo(OFF_SC // D_MODEL)),
            pl.BlockSpec((HALO_ROWS, D_MODEL), halo(OFF_SX // D_MODEL)),
            pl.BlockSpec((tm, D_MODEL), row(OFF_MG // D_MODEL)),
            pl.BlockSpec((tm, D_MODEL), row(OFF_MG // D_MODEL + 1)),
            pl.BlockSpec((tm, D_MODEL), row(OFF_MG // D_MODEL + 2)),
            pl.BlockSpec((tm, LANES), row(OFF_NG // LANES)),
            pl.BlockSpec((tm, D_MODEL), row(0)),
            pl.BlockSpec((tm, D_MODEL), row(0)),
            pl.BlockSpec((tm, D_MODEL), row(0)),
            pl.BlockSpec((CONV_WIDTH, D_MODEL), const),
            pl.BlockSpec((LANES, 3 * D_MODEL), const),
            wspec, wspec, wspec, wspec,
        ],
        out_specs=pl.BlockSpec((tm, D_MODEL), row(0)),
        out_shape=jax.ShapeDtypeStruct((t, D_MODEL), F32),
        compiler_params=_cparams(("arbitrary",)),
        name="merge",
    )(x2, y_ret, z, z, z, z, z, z, z, z, z, o_cmp, o_slc, o_win, conv_w, gate_expand,
      w_ret, w_sc, w_nsa, w_mix)


def _ffn_kernel(x_ref, nw_ref, wup_ref, cw_ref, wd_ref, fw_ref, out_ref, h_scr, ca_scr, cv_scr,
                *, tiles_per_seq, final_norm, tf):
    @pl.when((pl.program_id(0) % tiles_per_seq) == 0)
    def _():
        ca_scr[...] = jnp.zeros_like(ca_scr)
        cv_scr[...] = jnp.zeros_like(cv_scr)

    h_scr[...] = _rms(x_ref[...], nw_ref[...]).astype(BF16)
    out_ref[...] = x_ref[...]
    tm = h_scr.shape[0]
    nf = D_FF // tf
    cols_a = lambda j: slice(j * tf, (j + 1) * tf)
    cols_v = lambda j: slice(D_FF + j * tf, D_FF + (j + 1) * tf)

    def up(j):
        h = h_scr[...]
        return _dot(h, wup_ref[:, cols_a(j)]), _dot(h, wup_ref[:, cols_v(j)])

    def gate(j, ua, uv):
        a = _causal_conv3(ua, ca_scr[j], cw_ref[:, cols_a(j)])
        v = _causal_conv3(uv, cv_scr[j], cw_ref[:, cols_v(j)])
        ca_scr[j] = ua[tm - HALO_ROWS:, :]
        cv_scr[j] = uv[tm - HALO_ROWS:, :]
        return (a * _sigmoid(a) * v).astype(BF16)

    u_next = up(0)
    act_prev = None
    for j in range(nf):
        u_cur = u_next
        if j + 1 < nf:
            u_next = up(j + 1)
        act = gate(j, *u_cur)
        if act_prev is not None:
            out_ref[...] += _dot(act_prev, wd_ref[cols_a(j - 1), :])
        act_prev = act
    out_ref[...] += _dot(act_prev, wd_ref[cols_a(nf - 1), :])
    if final_norm:
        out_ref[...] = _rms(out_ref[...], fw_ref[...])


def _ffn(x2, norm_w, w_up, conv_w, w_down, final_w, layer, s, tm, tf, final_norm):
    t = x2.shape[0]
    nf = D_FF // tf
    const = lambda i: (0, 0)
    of_layer = lambda i: (layer, 0, 0)
    resident = dict(index_map=of_layer, pipeline_mode=pl.Buffered(1))
    return pl.pallas_call(
        functools.partial(_ffn_kernel, tiles_per_seq=s // tm, final_norm=final_norm, tf=tf),
        grid=(t // tm,),
        in_specs=[
            pl.BlockSpec((tm, D_MODEL), lambda i: (i, 0)),
            pl.BlockSpec((1, D_MODEL), const),
            pl.BlockSpec((None, D_MODEL, 2 * D_FF), **resident),
            pl.BlockSpec((None, CONV_WIDTH, 2 * D_FF), of_layer),
            pl.BlockSpec((None, D_FF, D_MODEL), **resident),
            pl.BlockSpec((1, D_MODEL), const),
        ],
        out_specs=pl.BlockSpec((tm, D_MODEL), lambda i: (i, 0)),
        out_shape=jax.ShapeDtypeStruct((t, D_MODEL), F32),
        scratch_shapes=[
            pltpu.VMEM((tm, D_MODEL), BF16),
            pltpu.VMEM((nf, HALO_ROWS, tf), F32),
            pltpu.VMEM((nf, HALO_ROWS, tf), F32),
        ],
        compiler_params=_cparams(("arbitrary",)),
        name="ffn",
    )(x2, norm_w, w_up, conv_w, w_down, final_w)


def _prep_w_in(w_in_l):
    w_in_l = w_in_l.astype(BF16)
    gw = NSA_KV_HEADS * NSA_DH
    src = OFF_MG
    k_c, v_c, k_s, v_s, k_w, v_w = [w_in_l[:, src + n * gw: src + (n + 1) * gw] for n in range(6)]
    ng = w_in_l[:, src + 6 * gw: src + 6 * gw + 3 * NSA_HEADS]
    ng = ng.reshape(D_MODEL, NSA_HEADS, 3).transpose(0, 2, 1).reshape(D_MODEL, 3 * NSA_HEADS)
    mg = w_in_l[:, src + 6 * gw + 3 * NSA_HEADS:]

    def lane_pad(v):
        v = v.reshape(D_MODEL, NSA_KV_HEADS, NSA_DH)
        return jnp.pad(v, ((0, 0), (0, 0), (0, LANES - NSA_DH))).reshape(D_MODEL, NSA_KV_HEADS * LANES)

    pad = jnp.zeros((D_MODEL, Z_WIDTH - OFF_NG - 3 * NSA_HEADS), w_in_l.dtype)
    return jnp.concatenate([w_in_l[:, :OFF_MG], mg, k_c, v_c, k_s, k_w, lane_pad(v_s), lane_pad(v_w), ng, pad],
                           axis=1).astype(BF16)


def _z_bias():
    bias = np.zeros((1, Z_WIDTH), np.float32)
    for off in (OFF_VS, OFF_VW):
        for g in range(NSA_KV_HEADS):
            bias[0, off + g * LANES + NSA_DH] = 1.0
    return jnp.asarray(bias)


def _gate_expander():
    e = np.zeros((LANES, 3 * D_MODEL), np.float32)
    for br in range(3):
        for h in range(NSA_HEADS):
            e[br * NSA_HEADS + h, br * D_MODEL + h * NSA_DH: br * D_MODEL + (h + 1) * NSA_DH] = 1.0
    return jnp.asarray(e, BF16)


def _overlap_t(s, ncp):
    nc = (s - CMP_BLOCK) // CMP_STRIDE + 1
    ns = s // SLC_BLOCK
    ci = np.arange(ncp) * CMP_STRIDE
    sj = np.arange(ns) * SLC_BLOCK
    ov = ((ci[None, :] < sj[:, None] + SLC_BLOCK) & (ci[None, :] + CMP_BLOCK > sj[:, None])
          & (np.arange(ncp)[None, :] < nc))
    return jnp.asarray(ov.astype(np.float32), BF16)


def kernel(x, attn_norm_w, w_in, ret_norm_w, w_ret_out, sc_conv_w, w_sc_out, nsa_cmp_pos, nsa_cmp_w1,
           nsa_cmp_w2, w_nsa_out, w_mix_out, ffn_norm_w, w_ffn_up, ffn_conv_w, w_ffn_down, final_norm_w):
    b, s, _ = x.shape
    t = b * s
    depth = w_in.shape[0]
    bg = b * NSA_KV_HEADS
    ncp = s // CMP_STRIDE
    assert s % 256 == 0 and s // SLC_BLOCK <= NSA_DH

    tm_in = min(1024, t)
    tm_merge = 512
    tm_ffn = min(1024, s)
    tq_cmp = 512
    tq_sel = 1024
    tk_sel = 512
    td_sel = 512
    tq_win = 256
    tr = 512

    ret_consts = _retention_consts(s)
    gate_expand = _gate_expander()
    z_bias = _z_bias()
    ovt = _overlap_t(s, ncp)
    onehot = np.zeros((s, LANES), np.float32)
    onehot[np.arange(s), (np.arange(s) // SLC_BLOCK) % NSA_DH] = 1.0
    blk_onehot = jnp.asarray(onehot, BF16)
    w_ret_bf, w_sc_bf, w_nsa_bf, w_mix_bf, w_up_bf, w_down_bf = (
        w.astype(BF16) for w in (w_ret_out, w_sc_out, w_nsa_out, w_mix_out, w_ffn_up, w_ffn_down))

    x2 = x.reshape(t, D_MODEL)
    for l in range(depth):
        z = _inproj(x2, attn_norm_w[l][None], _prep_w_in(w_in[l]), z_bias, tm_in, Z_TILE)
        y_ret = _retention(z, ret_norm_w[l][None], ret_consts, b, s, tr)

        kv_c = z[:, OFF_KC:OFF_KC + 2 * NSA_KV_HEADS * NSA_DH]
        kv_c = kv_c.reshape(b, ncp, CMP_STRIDE, 2, NSA_KV_HEADS, NSA_DH).transpose(3, 0, 4, 1, 2, 5)
        kv_c = kv_c.reshape(2, bg, ncp, CMP_STRIDE * NSA_DH)
        pos_flat = nsa_cmp_pos[l].reshape(2, 1, CMP_BLOCK * NSA_DH)
        w2p = jnp.pad(nsa_cmp_w2[l], ((0, 0), (0, 0), (0, LANES - NSA_DH))).astype(BF16)
        kvc = _compress(kv_c, pos_flat, nsa_cmp_w1[l].astype(BF16), w2p)

        o_cmp, sel_bias = _cmp_select(z, kvc, ovt, b, s, tq_cmp)
        o_slc = _flash(z, sel_bias, blk_onehot, b, s, tq_sel, tk_sel, td_sel, "sel")
        o_win = _flash(z, None, None, b, s, tq_win, None, None, "win")

        x2 = _merge(x2, z, y_ret, o_cmp, o_slc, o_win, sc_conv_w[l], gate_expand,
                    w_ret_bf, w_sc_bf, w_nsa_bf, w_mix_bf, l, s, tm_merge)
        x2 = _ffn(x2, ffn_norm_w[l][None], w_up_bf, ffn_conv_w, w_down_bf, final_norm_w[None],
                  l, s, tm_ffn, 256, l == depth - 1)
    return x2.reshape(b, s, D_MODEL)
```

```python
import functools

import numpy as np
import jax
import jax.numpy as jnp
from jax import lax
from jax.experimental import pallas as pl
from jax.experimental.pallas import tpu as pltpu

F32 = jnp.float32
BF16 = jnp.bfloat16

D_MODEL = 1024
RET_HEADS = 4
RET_DV = D_MODEL // RET_HEADS
RET_DK = RET_DV // 2
RET_CHUNK = 128
CONV_WIDTH = 3
NSA_DH = 64
NSA_HEADS = D_MODEL // NSA_DH
NSA_KV_HEADS = 4
NSA_HPG = NSA_HEADS // NSA_KV_HEADS
CMP_BLOCK = 32
CMP_STRIDE = 16
CMP_HIDDEN = 256
SLC_BLOCK = 64
N_SELECT = 16
WINDOW = 512
D_FF = ((8 * D_MODEL // 3 + 127) // 128) * 128
EPS = 1e-6
NEG = -1e30
FORCE = 1e6

WIN_STACK = 2
LANES = 128
HALO_ROWS = 8

OFF_RQ = 0
OFF_RK = 512
OFF_RV = 1024
OFF_RG = 2048
OFF_SB = 3072
OFF_SC = 4096
OFF_SX = 5120
OFF_NQ = 6144
OFF_MG = 7168
OFF_KC = 10240
OFF_VC = 10496
OFF_KS = 10752
OFF_KW = 11008
OFF_VS = 11264
OFF_VW = 11776
OFF_NG = 12288
Z_WIDTH = 12800
Z_TILE = 2560

VMEM_LIMIT = 56 * 1024 * 1024


def _cparams(sem):
    return pltpu.CompilerParams(dimension_semantics=sem, vmem_limit_bytes=VMEM_LIMIT)


def _nt_dot(a, b):
    return lax.dot_general(a, b, (((1,), (1,)), ((), ())), preferred_element_type=F32)


def _dot(a, b):
    return jnp.dot(a, b, preferred_element_type=F32)


def _sigmoid(x):
    return 1.0 / (1.0 + jnp.exp(-x))


def _rms(x, w):
    return x * lax.rsqrt(jnp.mean(x * x, axis=-1, keepdims=True) + EPS) * w


def _inproj_kernel(x_ref, nw_ref, w_ref, b_ref, z_ref, h_scr):
    @pl.when(pl.program_id(1) == 0)
    def _():
        h_scr[...] = _rms(x_ref[...], nw_ref[...]).astype(BF16)

    z_ref[...] = (_dot(h_scr[...], w_ref[...]) + b_ref[...]).astype(BF16)


def _inproj(x2, norm_w, w_in_p, z_bias, tm, tn):
    t = x2.shape[0]
    return pl.pallas_call(
        _inproj_kernel,
        grid=(t // tm, Z_WIDTH // tn),
        in_specs=[
            pl.BlockSpec((tm, D_MODEL), lambda i, j: (i, 0)),
            pl.BlockSpec((1, D_MODEL), lambda i, j: (0, 0)),
            pl.BlockSpec((D_MODEL, tn), lambda i, j: (0, j)),
            pl.BlockSpec((1, tn), lambda i, j: (0, j)),
        ],
        out_specs=pl.BlockSpec((tm, tn), lambda i, j: (i, j)),
        out_shape=jax.ShapeDtypeStruct((t, Z_WIDTH), BF16),
        scratch_shapes=[pltpu.VMEM((tm, D_MODEL), BF16)],
        compiler_params=_cparams(("arbitrary", "arbitrary")),
        name="inproj",
    )(x2, norm_w, w_in_p, z_bias)


def _ret_kernel(q_ref, k_ref, v_ref, g_ref, cos_ref, sin_ref, dm_ref, zeta_ref, xi_ref,
                dec_ref, nw_ref, y_ref, st_scr, *, n_chunks):
    @pl.when(pl.program_id(1) == 0)
    def _():
        st_scr[...] = jnp.zeros_like(st_scr)

    c = RET_CHUNK
    for ci in range(n_chunks):
        rows = slice(ci * c, (ci + 1) * c)
        cos = cos_ref[rows, :]
        sin = sin_ref[rows, :]
        for h in range(RET_HEADS):
            q = q_ref[rows, h * RET_DK:(h + 1) * RET_DK].astype(F32)
            k = k_ref[rows, h * RET_DK:(h + 1) * RET_DK].astype(F32)
            qr = q * cos + pltpu.roll(q, RET_DK // 2, axis=1) * sin
            kr = (k * cos + pltpu.roll(k, RET_DK // 2, axis=1) * sin) * (RET_DK ** -0.5)
            qb = qr.astype(BF16)
            kb = kr.astype(BF16)
            v = v_ref[rows, h * RET_DV:(h + 1) * RET_DV]
            scores = _nt_dot(qb, kb) * dm_ref[h]
            o = _dot(scores.astype(BF16), v)
            st = st_scr[h]
            o = o + _dot(qb, st.astype(BF16)) * xi_ref[h]
            vz = (v.astype(F32) * zeta_ref[h]).astype(BF16)
            kv = _dot(kr.T.astype(BF16), vz)
            st_scr[h] = st * dec_ref[h] + kv
            mu = jnp.mean(o, axis=-1, keepdims=True)
            d = o - mu
            var = jnp.mean(d * d, axis=-1, keepdims=True)
            on = d * lax.rsqrt(var + EPS) * nw_ref[:, h * RET_DV:(h + 1) * RET_DV]
            g = g_ref[rows, h * RET_DV:(h + 1) * RET_DV].astype(F32)
            y_ref[rows, h * RET_DV:(h + 1) * RET_DV] = (g * _sigmoid(g) * on).astype(BF16)


def _retention_consts(s):
    c = RET_CHUNK
    pos = jnp.arange(s, dtype=F32)
    theta = 10000.0 ** (-jnp.linspace(0.0, 1.0, RET_DK // 2, dtype=F32))
    ang = pos[:, None] * theta[None, :]
    cos = jnp.cos(ang)
    sin = jnp.sin(ang)
    cos2 = jnp.concatenate([cos, cos], axis=-1)
    sin2 = jnp.concatenate([-sin, sin], axis=-1)
    log_gamma = jnp.log1p(-(2.0 ** (-5.0 - jnp.arange(RET_HEADS, dtype=F32))))
    j = jnp.arange(c, dtype=F32)
    rel = j[:, None] - j[None, :]
    dmask = jnp.where(rel >= 0, jnp.exp(log_gamma[:, None, None] * jnp.maximum(rel, 0.0)), 0.0)
    zeta = jnp.exp(log_gamma[:, None] * (c - 1 - j)[None, :])
    xi = jnp.exp(log_gamma[:, None] * (j + 1.0)[None, :])
    dec = jnp.exp(log_gamma * c)
    zeta_b = jnp.broadcast_to(zeta[:, :, None], (RET_HEADS, c, RET_DV))
    xi_b = jnp.broadcast_to(xi[:, :, None], (RET_HEADS, c, RET_DV))
    dec_b = jnp.broadcast_to(dec[:, None, None], (RET_HEADS, RET_DK, RET_DV))
    return cos2, sin2, dmask, zeta_b, xi_b, dec_b


def _retention(z, ret_norm_w, consts, b, s, tr):
    t = b * s
    nt = s // tr
    cos2, sin2, dmask, zeta_b, xi_b, dec_b = consts
    const3 = lambda bi, i: (0, 0, 0)
    return pl.pallas_call(
        functools.partial(_ret_kernel, n_chunks=tr // RET_CHUNK),
        grid=(b, nt),
        in_specs=[
            pl.BlockSpec((tr, 512), lambda bi, i: (bi * nt + i, OFF_RQ // 512)),
            pl.BlockSpec((tr, 512), lambda bi, i: (bi * nt + i, OFF_RK // 512)),
            pl.BlockSpec((tr, 1024), lambda bi, i: (bi * nt + i, OFF_RV // 1024)),
            pl.BlockSpec((tr, 1024), lambda bi, i: (bi * nt + i, OFF_RG // 1024)),
            pl.BlockSpec((tr, RET_DK), lambda bi, i: (i, 0)),
            pl.BlockSpec((tr, RET_DK), lambda bi, i: (i, 0)),
            pl.BlockSpec((RET_HEADS, RET_CHUNK, RET_CHUNK), const3),
            pl.BlockSpec((RET_HEADS, RET_CHUNK, RET_DV), const3),
            pl.BlockSpec((RET_HEADS, RET_CHUNK, RET_DV), const3),
            pl.BlockSpec((RET_HEADS, RET_DK, RET_DV), const3),
            pl.BlockSpec((1, D_MODEL), lambda bi, i: (0, 0)),
        ],
        out_specs=pl.BlockSpec((tr, D_MODEL), lambda bi, i: (bi * nt + i, 0)),
        out_shape=jax.ShapeDtypeStruct((t, D_MODEL), BF16),
        scratch_shapes=[pltpu.VMEM((RET_HEADS, RET_DK, RET_DV), F32)],
        compiler_params=_cparams(("arbitrary", "arbitrary")),
        name="retention",
    )(z, z, z, z, cos2, sin2, dmask, zeta_b, xi_b, dec_b, ret_norm_w)


def _compress_kernel(rows_ref, pos_ref, w1_ref, w2_ref, out_ref):
    half = rows_ref.shape[-1]
    x = rows_ref[0, 0].astype(F32)
    lo = _dot((x + pos_ref[0, :, :half]).astype(BF16), w1_ref[0, :half, :])
    hi = _dot((x + pos_ref[0, :, half:]).astype(BF16), w1_ref[0, half:, :])
    hid = lo + pltpu.roll(hi, hi.shape[0] - 1, axis=0)
    gel = 0.5 * hid * (1.0 + jnp.tanh(np.sqrt(2.0 / np.pi) * (hid + 0.044715 * (hid * hid * hid))))
    out_ref[0, 0] = _dot(gel.astype(BF16), w2_ref[0]).astype(BF16)


def _compress(rows, pos_flat, w1, w2p):
    _, bg, ncp, half = rows.shape
    width = 2 * half
    return pl.pallas_call(
        _compress_kernel,
        grid=(2, bg),
        in_specs=[
            pl.BlockSpec((1, 1, ncp, half), lambda a, n: (a, n, 0, 0)),
            pl.BlockSpec((1, 1, width), lambda a, n: (a, 0, 0)),
            pl.BlockSpec((1, width, CMP_HIDDEN), lambda a, n: (a, 0, 0)),
            pl.BlockSpec((1, CMP_HIDDEN, LANES), lambda a, n: (a, 0, 0)),
        ],
        out_specs=pl.BlockSpec((1, 1, ncp, LANES), lambda a, n: (a, n, 0, 0)),
        out_shape=jax.ShapeDtypeStruct((2, bg, ncp, LANES), BF16),
        compiler_params=_cparams(("arbitrary", "arbitrary")),
        name="nsa_compress",
    )(rows, pos_flat, w1, w2p)


def _head_pad(q2_ref_block, h):
    x = q2_ref_block[:, (h // 2) * LANES:(h // 2 + 1) * LANES].astype(F32)
    if h % 2:
        x = pltpu.roll(x, NSA_DH, axis=1)
    lane = lax.broadcasted_iota(jnp.int32, x.shape, 1)
    return jnp.where(lane < NSA_DH, x, 0.0)


def _merge_heads(parts):
    lo = parts[0] + pltpu.roll(parts[1], NSA_DH, axis=1)
    hi = parts[2] + pltpu.roll(parts[3], NSA_DH, axis=1)
    return jnp.concatenate([lo, hi], axis=1)


def _cmp_select_kernel(q_ref, kc_ref, vc_ref, ovt_ref, ocmp_ref, bias_ref, *, tq, n_blk):
    for ti in range(q_ref.shape[0] // tq):
        _cmp_select_tile(ti * tq, q_ref, kc_ref, vc_ref, ovt_ref, ocmp_ref, bias_ref, tq, n_blk)


def _cmp_select_tile(t0, q_ref, kc_ref, vc_ref, ovt_ref, ocmp_ref, bias_ref, tq, n_blk_all):
    rows = slice(t0, t0 + tq)
    ncp = min(kc_ref.shape[2], -(-((t0 + tq) // CMP_STRIDE) // LANES) * LANES)
    n_blk = min(n_blk_all, -(-((t0 + tq) // SLC_BLOCK) // 8) * 8)
    kc = kc_ref[0, 0, :ncp, :]
    vc = vc_ref[0, 0, :ncp, :]
    ovt = ovt_ref[:n_blk, :ncp]
    scale = NSA_DH ** -0.5

    t_q = t0 + lax.broadcasted_iota(jnp.int32, (tq, ncp), 0)
    c_id = lax.broadcasted_iota(jnp.int32, (tq, ncp), 1)
    valid = (c_id * CMP_STRIDE + (CMP_BLOCK - 1)) <= t_q
    t_col = t0 + lax.broadcasted_iota(jnp.int32, (tq, 1), 0)
    any1 = jnp.where(t_col >= CMP_BLOCK - 1, 1.0, 0.0)

    imp = jnp.zeros((n_blk, tq), F32)
    o_parts = []
    for h in range(NSA_HPG):
        qp = (_head_pad(q_ref.at[rows, :], h) * scale).astype(BF16)
        s1 = jnp.where(valid, _nt_dot(qp, kc), NEG)
        m = jnp.max(s1, axis=-1, keepdims=True)
        e = jnp.exp(s1 - m)
        pb = (e * ((1.0 / jnp.sum(e, axis=-1, keepdims=True)) * any1)).astype(BF16)
        o_parts.append(_dot(pb, vc))
        imp = imp + _nt_dot(ovt, pb)
    ocmp_ref[rows, :] = _merge_heads(o_parts).astype(BF16)

    blk = lax.broadcasted_iota(jnp.int32, (n_blk, tq), 0)
    t_row = t0 + lax.broadcasted_iota(jnp.int32, (n_blk, tq), 1)
    cur = t_row // SLC_BLOCK
    causal = blk <= cur
    forced = (blk == 0) | (blk == cur) | (blk == cur - 1)
    imp = jnp.where(forced, FORCE, imp)
    imp = jnp.where(causal, imp, NEG)
    sub = 8
    groups = [imp[g * sub:(g + 1) * sub, :] for g in range(n_blk // sub)]
    ranks = [jnp.zeros((sub, tq), F32) for _ in groups]
    row_in_group = lax.broadcasted_iota(jnp.int32, (sub, tq), 0)
    for i in range(n_blk):
        r = jnp.broadcast_to(imp[i:i + 1, :], (sub, tq))
        for g, x in enumerate(groups):
            if g < i // sub:
                ahead = jnp.where(r > x, 1.0, 0.0)
            elif g > i // sub:
                ahead = jnp.where(r >= x, 1.0, 0.0)
            else:
                ahead = jnp.where(row_in_group > i % sub, jnp.where(r >= x, 1.0, 0.0), jnp.where(r > x, 1.0, 0.0))
            ranks[g] = ranks[g] + ahead
    rank = jnp.concatenate(ranks, axis=0)
    bias_t = jnp.where(rank < float(min(N_SELECT, n_blk_all)), 0.0, NEG)
    pieces = [bias_t]
    if n_blk < NSA_DH:
        pieces.append(jnp.full((NSA_DH - n_blk, tq), NEG, F32))
    pieces.append(jnp.zeros((LANES - NSA_DH, tq), F32))
    bias_ref[0, rows, :] = jnp.concatenate(pieces, axis=0).T.astype(BF16)


def _cmp_select(z, kvc, ovt, b, s, tq):
    t = b * s
    bg = b * NSA_KV_HEADS
    ncp = kvc.shape[2]
    n_blk = s // SLC_BLOCK
    qcol = OFF_NQ // 256
    return pl.pallas_call(
        functools.partial(_cmp_select_kernel, tq=tq, n_blk=n_blk),
        grid=(bg,),
        in_specs=[
            pl.BlockSpec((s, 256), lambda n: (n // NSA_KV_HEADS, qcol + n % NSA_KV_HEADS)),
            pl.BlockSpec((1, 1, ncp, LANES), lambda n: (0, n, 0, 0)),
            pl.BlockSpec((1, 1, ncp, LANES), lambda n: (1, n, 0, 0)),
            pl.BlockSpec((n_blk, ncp), lambda n: (0, 0)),
        ],
        out_specs=[
            pl.BlockSpec((s, 256), lambda n: (n // NSA_KV_HEADS, n % NSA_KV_HEADS)),
            pl.BlockSpec((1, s, LANES), lambda n: (n, 0, 0)),
        ],
        out_shape=[
            jax.ShapeDtypeStruct((t, D_MODEL), BF16),
            jax.ShapeDtypeStruct((bg, s, LANES), BF16),
        ],
        compiler_params=_cparams(("arbitrary",)),
        name="nsa_cmp_select",
    )(z, kvc, kvc, ovt)


def _normalised_heads(acc_of_head):
    outs = []
    for h in range(NSA_HPG):
        acc = acc_of_head(h)
        lane = lax.broadcasted_iota(jnp.int32, acc.shape, 1)
        denom = jnp.sum(jnp.where(lane == NSA_DH, acc, 0.0), axis=-1, keepdims=True)
        outs.append(jnp.where(lane < NSA_DH, acc * (1.0 / denom), 0.0))
    return _merge_heads(outs).astype(BF16)


def _q_pair(q_ref, h, odd_group):
    x = q_ref[:, (h // 2) * LANES:(h // 2 + 1) * LANES].astype(F32) * (NSA_DH ** -0.5)
    odd_head = jnp.bool_(h % 2 == 1)
    x = jnp.where(odd_head != odd_group, pltpu.roll(x, NSA_DH, axis=1), x)
    lane_half = lax.broadcasted_iota(jnp.int32, x.shape, 1) // NSA_DH
    return jnp.where(lane_half == odd_group.astype(jnp.int32), x, 0.0).astype(BF16)


def _flash_sel_kernel(q_ref, b_ref, k_ref, oh_ref, v_ref, o_ref, qa_scr, m_scr, acc_scr, *, tq, tk, td):
    t0 = pl.program_id(1) * tq
    odd_group = (pl.program_id(0) % 2) == 1
    for h in range(NSA_HPG):
        qa_scr[h] = jnp.concatenate([_q_pair(q_ref, h, odd_group), b_ref[0]], axis=1)
    m_scr[...] = jnp.full(m_scr.shape, NEG, F32)
    acc_scr[...] = jnp.zeros(acc_scr.shape, F32)

    def tile(start, row0, row1, width, masked):
        k = jnp.concatenate([k_ref[pl.ds(start, width), :], oh_ref[pl.ds(start, width), :]], axis=1)
        v = v_ref[pl.ds(start, width), :]
        nr = row1 - row0
        if masked:
            ok = (lax.broadcasted_iota(jnp.int32, (nr, width), 1)
                  <= lax.broadcasted_iota(jnp.int32, (nr, width), 0))
        scores = lambda h: _nt_dot(qa_scr[h, row0:row1, :], k)

        sc_next = scores(0)
        for h in range(NSA_HPG):
            sc = sc_next
            if h + 1 < NSA_HPG:
                sc_next = scores(h + 1)
            if masked:
                sc = jnp.where(ok, sc, NEG)
            m_prev = m_scr[h, row0:row1, :]
            m_new = jnp.maximum(m_prev, jnp.max(sc, axis=-1, keepdims=True))
            alpha = jnp.exp(m_prev - m_new)
            p = jnp.exp(sc - jnp.concatenate([m_new] * (width // LANES), axis=1))
            acc_scr[h, row0:row1, :] = alpha * acc_scr[h, row0:row1, :] + _dot(p.astype(BF16), v)
            m_scr[h, row0:row1, :] = m_new

    def body(kt, carry):
        tile(pl.multiple_of(kt * tk, tk), 0, tq, tk, False)
        return carry

    lax.fori_loop(0, t0 // tk, body, 0)
    for j in range(tq // td):
        tile(pl.multiple_of(t0 + j * td, td), j * td, tq, td, True)
    o_ref[...] = _normalised_heads(lambda h: acc_scr[h])


def _flash_win_kernel(q_ref, k_ref, v_ref, band_ref, o_ref, *, tq):
    t0 = pl.program_id(1) * tq
    odd_group = (pl.program_id(0) % 2) == 1
    n_tiles = WINDOW // tq + 1
    ks, vs, biases = [], [], []
    for j in range(n_tiles):
        first = t0 + (j - n_tiles + 1) * tq
        start = pl.multiple_of(jnp.maximum(first, 0), tq)
        ks.append(k_ref[pl.ds(start, tq), :])
        vs.append(v_ref[pl.ds(start, tq), :])
        biases.append(jnp.where(first < 0, NEG, band_ref[j]))

    def scores(h0):
        q = jnp.concatenate([_q_pair(q_ref, h0 + d, odd_group) for d in range(WIN_STACK)], axis=0)
        return [_nt_dot(q, ks[j]) + jnp.concatenate([biases[j]] * WIN_STACK, axis=0) for j in range(n_tiles)]

    accs = []
    scs_next = scores(0)
    for h0 in range(0, NSA_HPG, WIN_STACK):
        scs = scs_next
        if h0 + WIN_STACK < NSA_HPG:
            scs_next = scores(h0 + WIN_STACK)
        m = jnp.max(scs[0], axis=-1, keepdims=True)
        for sc in scs[1:]:
            m = jnp.maximum(m, jnp.max(sc, axis=-1, keepdims=True))
        acc = _dot(jnp.exp(scs[0] - m).astype(BF16), vs[0])
        for j in range(1, n_tiles):
            acc = acc + _dot(jnp.exp(scs[j] - m).astype(BF16), vs[j])
        accs.extend(acc[d * tq:(d + 1) * tq] for d in range(WIN_STACK))

    o_ref[...] = _normalised_heads(lambda h: accs[h])


def _flash(z, sel_bias, blk_onehot, b, s, tq, tk, td, mode):
    t = b * s
    nq = s // tq
    bg = b * NSA_KV_HEADS
    grp = lambda n: n % NSA_KV_HEADS
    bat = lambda n: n // NSA_KV_HEADS
    off_k, off_v = (OFF_KS, OFF_VS) if mode == "sel" else (OFF_KW, OFF_VW)
    q_spec = pl.BlockSpec((tq, 256), lambda n, i: (bat(n) * nq + i, OFF_NQ // 256 + grp(n)))
    k_spec = pl.BlockSpec((s, LANES), lambda n, i: (bat(n), off_k // LANES + grp(n) // 2))
    v_spec = pl.BlockSpec((s, LANES), lambda n, i: (bat(n), off_v // LANES + grp(n)))
    if mode == "sel":
        body = functools.partial(_flash_sel_kernel, tq=tq, tk=tk, td=td)
        in_specs = [q_spec, pl.BlockSpec((1, tq, LANES), lambda n, i: (n, i, 0)), k_spec,
                    pl.BlockSpec((s, LANES), lambda n, i: (0, 0)), v_spec]
        args = (z, sel_bias, z, blk_onehot, z)
        scratch = [pltpu.VMEM((NSA_HPG, tq, 2 * LANES), BF16), pltpu.VMEM((NSA_HPG, tq, LANES), F32),
                   pltpu.VMEM((NSA_HPG, tq, LANES), F32)]
    else:
        body = functools.partial(_flash_win_kernel, tq=tq)
        n_tiles = WINDOW // tq + 1
        rel = (np.arange(tq)[None, :, None] - np.arange(tq)[None, None, :]
               + (n_tiles - 1 - np.arange(n_tiles))[:, None, None] * tq)
        band = jnp.asarray(np.where((rel >= 0) & (rel < WINDOW), 0.0, NEG), F32)
        in_specs = [q_spec, k_spec, v_spec, pl.BlockSpec((n_tiles, tq, tq), lambda n, i: (0, 0, 0))]
        args = (z, z, z, band)
        scratch = []
    return pl.pallas_call(
        body,
        grid=(bg, nq),
        in_specs=in_specs,
        out_specs=pl.BlockSpec((tq, 256), lambda n, i: (bat(n) * nq + i, grp(n))),
        out_shape=jax.ShapeDtypeStruct((t, D_MODEL), BF16),
        scratch_shapes=scratch,
        compiler_params=_cparams(("arbitrary", "arbitrary")),
        name="nsa_flash_" + mode,
    )(*args)


def _shifted(u, prev, n):
    rolled = pltpu.roll(u, n, axis=0)
    head = rolled[:HALO_ROWS]
    row = lax.broadcasted_iota(jnp.int32, head.shape, 0)
    for r in range(n):
        head = jnp.where(row == r, prev[HALO_ROWS - n + r:HALO_ROWS - n + r + 1, :], head)
    return jnp.concatenate([head, rolled[HALO_ROWS:]], axis=0)


def _causal_conv3(u, prev, w):
    return w[0:1, :] * _shifted(u, prev, 2) + w[1:2, :] * _shifted(u, prev, 1) + w[2:3, :] * u


def _merge_kernel(x_ref, yret_ref, sb_ref, sc_ref, sx_ref, sch_ref, sxh_ref, mg0_ref, mg1_ref, mg2_ref,
                  ng_ref, ocmp_ref, oslc_ref, owin_ref, cw_ref, e_ref, wret_ref, wsc_ref, wnsa_ref,
                  wmix_ref, out_ref, *, tiles_per_seq):
    first = (pl.program_id(0) % tiles_per_seq) == 0
    u = sc_ref[...].astype(F32) * sx_ref[...].astype(F32)
    prev = sch_ref[...].astype(F32) * sxh_ref[...].astype(F32)
    prev = jnp.where(first, 0.0, prev)
    y_sc = (sb_ref[...].astype(F32) * _causal_conv3(u, prev, cw_ref[...])).astype(BF16)

    gates = _dot(_sigmoid(ng_ref[...].astype(F32)).astype(BF16), e_ref[...])
    y_nsa = (gates[:, 0:D_MODEL] * ocmp_ref[...].astype(F32)
             + gates[:, D_MODEL:2 * D_MODEL] * oslc_ref[...].astype(F32)
             + gates[:, 2 * D_MODEL:3 * D_MODEL] * owin_ref[...].astype(F32)).astype(BF16)

    merged = (_sigmoid(mg0_ref[...].astype(F32)) * _dot(yret_ref[...], wret_ref[...])
              + _sigmoid(mg1_ref[...].astype(F32)) * _dot(y_sc, wsc_ref[...])
              + _sigmoid(mg2_ref[...].astype(F32)) * _dot(y_nsa, wnsa_ref[...]))
    out_ref[...] = x_ref[...] + _dot(merged.astype(BF16), wmix_ref[...])


def _merge(x2, z, y_ret, o_cmp, o_slc, o_win, conv_w, gate_expand, w_ret, w_sc, w_nsa, w_mix, layer, s, tm):
    t = x2.shape[0]
    row = lambda c: (lambda i: (i, c))
    halo = lambda c: (lambda i: (jnp.maximum(i * (tm // HALO_ROWS) - 1, 0), c))
    const = lambda i: (0, 0)
    wspec = pl.BlockSpec((None, D_MODEL, D_MODEL), lambda i: (layer, 0, 0),
                         pipeline_mode=pl.Buffered(1))
    return pl.pallas_call(
        functools.partial(_merge_kernel, tiles_per_seq=s // tm),
        grid=(t // tm,),
        in_specs=[
            pl.BlockSpec((tm, D_MODEL), row(0)),
            pl.BlockSpec((tm, D_MODEL), row(0)),
            pl.BlockSpec((tm, D_MODEL), row(OFF_SB // D_MODEL)),
            pl.BlockSpec((tm, D_MODEL), row(OFF_SC // D_MODEL)),
            pl.BlockSpec((tm, D_MODEL), row(OFF_SX // D_MODEL)),
            pl.BlockSpec((HALO_ROWS, D_MODEL), halo(OFF_SC // D_MODEL)),
            pl.BlockSpec((HALO_ROWS, D_MODEL), halo(OFF_SX // D_MODEL)),
            pl.BlockSpec((tm, D_MODEL), row(OFF_MG // D_MODEL)),
            pl.BlockSpec((tm, D_MODEL), row(OFF_MG // D_MODEL + 1)),
            pl.BlockSpec((tm, D_MODEL), row(OFF_MG // D_MODEL + 2)),
            pl.BlockSpec((tm, LANES), row(OFF_NG // LANES)),
            pl.BlockSpec((tm, D_MODEL), row(0)),
            pl.BlockSpec((tm, D_MODEL), row(0)),
            pl.BlockSpec((tm, D_MODEL), row(0)),
            pl.BlockSpec((CONV_WIDTH, D_MODEL), const),
            pl.BlockSpec((LANES, 3 * D_MODEL), const),
            wspec, wspec, wspec, wspec,
        ],
        out_specs=pl.BlockSpec((tm, D_MODEL), row(0)),
        out_shape=jax.ShapeDtypeStruct((t, D_MODEL), F32),
        compiler_params=_cparams(("arbitrary",)),
        name="merge",
    )(x2, y_ret, z, z, z, z, z, z, z, z, z, o_cmp, o_slc, o_win, conv_w, gate_expand,
      w_ret, w_sc, w_nsa, w_mix)


def _ffn_kernel(x_ref, nw_ref, wup_ref, cw_ref, wd_ref, fw_ref, out_ref, h_scr, ca_scr, cv_scr,
                *, tiles_per_seq, final_norm, tf):
    @pl.when((pl.program_id(0) % tiles_per_seq) == 0)
    def _():
        ca_scr[...] = jnp.zeros_like(ca_scr)
        cv_scr[...] = jnp.zeros_like(cv_scr)

    h_scr[...] = _rms(x_ref[...], nw_ref[...]).astype(BF16)
    out_ref[...] = x_ref[...]
    tm = h_scr.shape[0]
    nf = D_FF // tf
    cols_a = lambda j: slice(j * tf, (j + 1) * tf)
    cols_v = lambda j: slice(D_FF + j * tf, D_FF + (j + 1) * tf)

    def up(j):
        h = h_scr[...]
        return _dot(h, wup_ref[:, cols_a(j)]), _dot(h, wup_ref[:, cols_v(j)])

    def gate(j, ua, uv):
        a = _causal_conv3(ua, ca_scr[j], cw_ref[:, cols_a(j)])
        v = _causal_conv3(uv, cv_scr[j], cw_ref[:, cols_v(j)])
        ca_scr[j] = ua[tm - HALO_ROWS:, :]
        cv_scr[j] = uv[tm - HALO_ROWS:, :]
        return (a * _sigmoid(a) * v).astype(BF16)

    u_next = up(0)
    act_prev = None
    for j in range(nf):
        u_cur = u_next
        if j + 1 < nf:
            u_next = up(j + 1)
        act = gate(j, *u_cur)
        if act_prev is not None:
            out_ref[...] += _dot(act_prev, wd_ref[cols_a(j - 1), :])
        act_prev = act
    out_ref[...] += _dot(act_prev, wd_ref[cols_a(nf - 1), :])
    if final_norm:
        out_ref[...] = _rms(out_ref[...], fw_ref[...])


def _ffn(x2, norm_w, w_up, conv_w, w_down, final_w, layer, s, tm, tf, final_norm):
    t = x2.shape[0]
    nf = D_FF // tf
    const = lambda i: (0, 0)
    of_layer = lambda i: (layer, 0, 0)
    resident = dict(index_map=of_layer, pipeline_mode=pl.Buffered(1))
    return pl.pallas_call(
        functools.partial(_ffn_kernel, tiles_per_seq=s // tm, final_norm=final_norm, tf=tf),
        grid=(t // tm,),
        in_specs=[
            pl.BlockSpec((tm, D_MODEL), lambda i: (i, 0)),
            pl.BlockSpec((1, D_MODEL), const),
            pl.BlockSpec((None, D_MODEL, 2 * D_FF), **resident),
            pl.BlockSpec((None, CONV_WIDTH, 2 * D_FF), of_layer),
            pl.BlockSpec((None, D_FF, D_MODEL), **resident),
            pl.BlockSpec((1, D_MODEL), const),
        ],
        out_specs=pl.BlockSpec((tm, D_MODEL), lambda i: (i, 0)),
        out_shape=jax.ShapeDtypeStruct((t, D_MODEL), F32),
        scratch_shapes=[
            pltpu.VMEM((tm, D_MODEL), BF16),
            pltpu.VMEM((nf, HALO_ROWS, tf), F32),
            pltpu.VMEM((nf, HALO_ROWS, tf), F32),
        ],
        compiler_params=_cparams(("arbitrary",)),
        name="ffn",
    )(x2, norm_w, w_up, conv_w, w_down, final_w)


def _prep_w_in(w_in_l):
    w_in_l = w_in_l.astype(BF16)
    gw = NSA_KV_HEADS * NSA_DH
    src = OFF_MG
    k_c, v_c, k_s, v_s, k_w, v_w = [w_in_l[:, src + n * gw: src + (n + 1) * gw] for n in range(6)]
    ng = w_in_l[:, src + 6 * gw: src + 6 * gw + 3 * NSA_HEADS]
    ng = ng.reshape(D_MODEL, NSA_HEADS, 3).transpose(0, 2, 1).reshape(D_MODEL, 3 * NSA_HEADS)
    mg = w_in_l[:, src + 6 * gw + 3 * NSA_HEADS:]

    def lane_pad(v):
        v = v.reshape(D_MODEL, NSA_KV_HEADS, NSA_DH)
        return jnp.pad(v, ((0, 0), (0, 0), (0, LANES - NSA_DH))).reshape(D_MODEL, NSA_KV_HEADS * LANES)

    pad = jnp.zeros((D_MODEL, Z_WIDTH - OFF_NG - 3 * NSA_HEADS), w_in_l.dtype)
    return jnp.concatenate([w_in_l[:, :OFF_MG], mg, k_c, v_c, k_s, k_w, lane_pad(v_s), lane_pad(v_w), ng, pad],
                           axis=1).astype(BF16)


def _z_bias():
    bias = np.zeros((1, Z_WIDTH), np.float32)
    for off in (OFF_VS, OFF_VW):
        for g in range(NSA_KV_HEADS):
            bias[0, off + g * LANES + NSA_DH] = 1.0
    return jnp.asarray(bias)


def _gate_expander():
    e = np.zeros((LANES, 3 * D_MODEL), np.float32)
    for br in range(3):
        for h in range(NSA_HEADS):
            e[br * NSA_HEADS + h, br * D_MODEL + h * NSA_DH: br * D_MODEL + (h + 1) * NSA_DH] = 1.0
    return jnp.asarray(e, BF16)


def _overlap_t(s, ncp):
    nc = (s - CMP_BLOCK) // CMP_STRIDE + 1
    ns = s // SLC_BLOCK
    ci = np.arange(ncp) * CMP_STRIDE
    sj = np.arange(ns) * SLC_BLOCK
    ov = ((ci[None, :] < sj[:, None] + SLC_BLOCK) & (ci[None, :] + CMP_BLOCK > sj[:, None])
          & (np.arange(ncp)[None, :] < nc))
    return jnp.asarray(ov.astype(np.float32), BF16)


def kernel(x, attn_norm_w, w_in, ret_norm_w, w_ret_out, sc_conv_w, w_sc_out, nsa_cmp_pos, nsa_cmp_w1,
           nsa_cmp_w2, w_nsa_out, w_mix_out, ffn_norm_w, w_ffn_up, ffn_conv_w, w_ffn_down, final_norm_w):
    b, s, _ = x.shape
    t = b * s
    depth = w_in.shape[0]
    bg = b * NSA_KV_HEADS
    ncp = s // CMP_STRIDE
    assert s % 256 == 0 and s // SLC_BLOCK <= NSA_DH

    tm_in = min(1024, t)
    tm_merge = 512
    tm_ffn = min(1024, s)
    tq_cmp = 512
    tq_sel = 1024
    tk_sel = 512
    td_sel = 512
    tq_win = 512
    tr = 512

    ret_consts = _retention_consts(s)
    gate_expand = _gate_expander()
    z_bias = _z_bias()
    ovt = _overlap_t(s, ncp)
    onehot = np.zeros((s, LANES), np.float32)
    onehot[np.arange(s), (np.arange(s) // SLC_BLOCK) % NSA_DH] = 1.0
    blk_onehot = jnp.asarray(onehot, BF16)
    w_ret_bf, w_sc_bf, w_nsa_bf, w_mix_bf, w_up_bf, w_down_bf = (
        w.astype(BF16) for w in (w_ret_out, w_sc_out, w_nsa_out, w_mix_out, w_ffn_up, w_ffn_down))

    x2 = x.reshape(t, D_MODEL)
    for l in range(depth):
        z = _inproj(x2, attn_norm_w[l][None], _prep_w_in(w_in[l]), z_bias, tm_in, Z_TILE)
        y_ret = _retention(z, ret_norm_w[l][None], ret_consts, b, s, tr)

        kv_c = z[:, OFF_KC:OFF_KC + 2 * NSA_KV_HEADS * NSA_DH]
        kv_c = kv_c.reshape(b, ncp, CMP_STRIDE, 2, NSA_KV_HEADS, NSA_DH).transpose(3, 0, 4, 1, 2, 5)
        kv_c = kv_c.reshape(2, bg, ncp, CMP_STRIDE * NSA_DH)
        pos_flat = nsa_cmp_pos[l].reshape(2, 1, CMP_BLOCK * NSA_DH)
        w2p = jnp.pad(nsa_cmp_w2[l], ((0, 0), (0, 0), (0, LANES - NSA_DH))).astype(BF16)
        kvc = _compress(kv_c, pos_flat, nsa_cmp_w1[l].astype(BF16), w2p)

        o_cmp, sel_bias = _cmp_select(z, kvc, ovt, b, s, tq_cmp)
        o_slc = _flash(z, sel_bias, blk_onehot, b, s, tq_sel, tk_sel, td_sel, "sel")
        o_win = _flash(z, None, None, b, s, tq_win, None, None, "win")

        x2 = _merge(x2, z, y_ret, o_cmp, o_slc, o_win, sc_conv_w[l], gate_expand,
                    w_ret_bf, w_sc_bf, w_nsa_bf, w_mix_bf, l, s, tm_merge)
        x2 = _ffn(x2, ffn_norm_w[l][None], w_up_bf, ffn_conv_w, w_down_bf, final_norm_w[None],
                  l, s, tm_ffn, 256, l == depth - 1)
    return x2.reshape(b, s, D_MODEL)
```

```python
import functools

import numpy as np
import jax
import jax.numpy as jnp
from jax import lax
from jax.experimental import pallas as pl
from jax.experimental.pallas import tpu as pltpu

F32 = jnp.float32
BF16 = jnp.bfloat16

D_MODEL = 1024
RET_HEADS = 4
RET_DV = D_MODEL // RET_HEADS
RET_DK = RET_DV // 2
RET_CHUNK = 128
CONV_WIDTH = 3
NSA_DH = 64
NSA_HEADS = D_MODEL // NSA_DH
NSA_KV_HEADS = 4
NSA_HPG = NSA_HEADS // NSA_KV_HEADS
CMP_BLOCK = 32
CMP_STRIDE = 16
CMP_HIDDEN = 256
SLC_BLOCK = 64
N_SELECT = 16
WINDOW = 512
D_FF = ((8 * D_MODEL // 3 + 127) // 128) * 128
EPS = 1e-6
NEG = -1e30
FORCE = 1e6

FFN_DOWN_GROUP = 6
WIN_STACK = 2
LANES = 128
HALO_ROWS = 8

OFF_RQ = 0
OFF_RK = 512
OFF_RV = 1024
OFF_RG = 2048
OFF_SB = 3072
OFF_SC = 4096
OFF_SX = 5120
OFF_NQ = 6144
OFF_MG = 7168
OFF_KC = 10240
OFF_VC = 10496
OFF_KS = 10752
OFF_KW = 11008
OFF_VS = 11264
OFF_VW = 11776
OFF_NG = 12288
Z_WIDTH = 12800
Z_TILE = 2560

VMEM_LIMIT = 56 * 1024 * 1024


def _cparams(sem):
    return pltpu.CompilerParams(dimension_semantics=sem, vmem_limit_bytes=VMEM_LIMIT)


def _nt_dot(a, b):
    return lax.dot_general(a, b, (((1,), (1,)), ((), ())), preferred_element_type=F32)


def _dot(a, b):
    return jnp.dot(a, b, preferred_element_type=F32)


def _sigmoid(x):
    return 1.0 / (1.0 + jnp.exp(-x))


def _rms(x, w):
    return x * lax.rsqrt(jnp.mean(x * x, axis=-1, keepdims=True) + EPS) * w


def _inproj_kernel(x_ref, nw_ref, w_ref, b_ref, z_ref, h_scr):
    @pl.when(pl.program_id(1) == 0)
    def _():
        h_scr[...] = _rms(x_ref[...], nw_ref[...]).astype(BF16)

    z_ref[...] = (_dot(h_scr[...], w_ref[...]) + b_ref[...]).astype(BF16)


def _inproj(x2, norm_w, w_in_p, z_bias, tm, tn):
    t = x2.shape[0]
    return pl.pallas_call(
        _inproj_kernel,
        grid=(t // tm, Z_WIDTH // tn),
        in_specs=[
            pl.BlockSpec((tm, D_MODEL), lambda i, j: (i, 0)),
            pl.BlockSpec((1, D_MODEL), lambda i, j: (0, 0)),
            pl.BlockSpec((D_MODEL, tn), lambda i, j: (0, j)),
            pl.BlockSpec((1, tn), lambda i, j: (0, j)),
        ],
        out_specs=pl.BlockSpec((tm, tn), lambda i, j: (i, j)),
        out_shape=jax.ShapeDtypeStruct((t, Z_WIDTH), BF16),
        scratch_shapes=[pltpu.VMEM((tm, D_MODEL), BF16)],
        compiler_params=_cparams(("arbitrary", "arbitrary")),
        name="inproj",
    )(x2, norm_w, w_in_p, z_bias)


def _ret_kernel(q_ref, k_ref, v_ref, g_ref, cos_ref, sin_ref, dm_ref, zeta_ref, xi_ref,
                dec_ref, nw_ref, y_ref, st_scr, *, n_chunks):
    @pl.when(pl.program_id(1) == 0)
    def _():
        st_scr[...] = jnp.zeros_like(st_scr)

    c = RET_CHUNK
    for ci in range(n_chunks):
        rows = slice(ci * c, (ci + 1) * c)
        cos = cos_ref[rows, :]
        sin = sin_ref[rows, :]
        for h in range(RET_HEADS):
            q = q_ref[rows, h * RET_DK:(h + 1) * RET_DK].astype(F32)
            k = k_ref[rows, h * RET_DK:(h + 1) * RET_DK].astype(F32)
            qr = q * cos + pltpu.roll(q, RET_DK // 2, axis=1) * sin
            kr = (k * cos + pltpu.roll(k, RET_DK // 2, axis=1) * sin) * (RET_DK ** -0.5)
            qb = qr.astype(BF16)
            kb = kr.astype(BF16)
            v = v_ref[rows, h * RET_DV:(h + 1) * RET_DV]
            scores = _nt_dot(qb, kb) * dm_ref[h]
            o = _dot(scores.astype(BF16), v)
            st = st_scr[h]
            o = o + _dot(qb, st.astype(BF16)) * xi_ref[h]
            vz = (v.astype(F32) * zeta_ref[h]).astype(BF16)
            kv = _dot(kr.T.astype(BF16), vz)
            st_scr[h] = st * dec_ref[h] + kv
            mu = jnp.mean(o, axis=-1, keepdims=True)
            d = o - mu
            var = jnp.mean(d * d, axis=-1, keepdims=True)
            on = d * lax.rsqrt(var + EPS) * nw_ref[:, h * RET_DV:(h + 1) * RET_DV]
            g = g_ref[rows, h * RET_DV:(h + 1) * RET_DV].astype(F32)
            y_ref[rows, h * RET_DV:(h + 1) * RET_DV] = (g * _sigmoid(g) * on).astype(BF16)


def _retention_consts(s):
    c = RET_CHUNK
    pos = jnp.arange(s, dtype=F32)
    theta = 10000.0 ** (-jnp.linspace(0.0, 1.0, RET_DK // 2, dtype=F32))
    ang = pos[:, None] * theta[None, :]
    cos = jnp.cos(ang)
    sin = jnp.sin(ang)
    cos2 = jnp.concatenate([cos, cos], axis=-1)
    sin2 = jnp.concatenate([-sin, sin], axis=-1)
    log_gamma = jnp.log1p(-(2.0 ** (-5.0 - jnp.arange(RET_HEADS, dtype=F32))))
    j = jnp.arange(c, dtype=F32)
    rel = j[:, None] - j[None, :]
    dmask = jnp.where(rel >= 0, jnp.exp(log_gamma[:, None, None] * jnp.maximum(rel, 0.0)), 0.0)
    zeta = jnp.exp(log_gamma[:, None] * (c - 1 - j)[None, :])
    xi = jnp.exp(log_gamma[:, None] * (j + 1.0)[None, :])
    dec = jnp.exp(log_gamma * c)
    zeta_b = jnp.broadcast_to(zeta[:, :, None], (RET_HEADS, c, RET_DV))
    xi_b = jnp.broadcast_to(xi[:, :, None], (RET_HEADS, c, RET_DV))
    dec_b = jnp.broadcast_to(dec[:, None, None], (RET_HEADS, RET_DK, RET_DV))
    return cos2, sin2, dmask, zeta_b, xi_b, dec_b


def _retention(z, ret_norm_w, consts, b, s, tr):
    t = b * s
    nt = s // tr
    cos2, sin2, dmask, zeta_b, xi_b, dec_b = consts
    const3 = lambda bi, i: (0, 0, 0)
    return pl.pallas_call(
        functools.partial(_ret_kernel, n_chunks=tr // RET_CHUNK),
        grid=(b, nt),
        in_specs=[
            pl.BlockSpec((tr, 512), lambda bi, i: (bi * nt + i, OFF_RQ // 512)),
            pl.BlockSpec((tr, 512), lambda bi, i: (bi * nt + i, OFF_RK // 512)),
            pl.BlockSpec((tr, 1024), lambda bi, i: (bi * nt + i, OFF_RV // 1024)),
            pl.BlockSpec((tr, 1024), lambda bi, i: (bi * nt + i, OFF_RG // 1024)),
            pl.BlockSpec((tr, RET_DK), lambda bi, i: (i, 0)),
            pl.BlockSpec((tr, RET_DK), lambda bi, i: (i, 0)),
            pl.BlockSpec((RET_HEADS, RET_CHUNK, RET_CHUNK), const3),
            pl.BlockSpec((RET_HEADS, RET_CHUNK, RET_DV), const3),
            pl.BlockSpec((RET_HEADS, RET_CHUNK, RET_DV), const3),
            pl.BlockSpec((RET_HEADS, RET_DK, RET_DV), const3),
            pl.BlockSpec((1, D_MODEL), lambda bi, i: (0, 0)),
        ],
        out_specs=pl.BlockSpec((tr, D_MODEL), lambda bi, i: (bi * nt + i, 0)),
        out_shape=jax.ShapeDtypeStruct((t, D_MODEL), BF16),
        scratch_shapes=[pltpu.VMEM((RET_HEADS, RET_DK, RET_DV), F32)],
        compiler_params=_cparams(("arbitrary", "arbitrary")),
        name="retention",
    )(z, z, z, z, cos2, sin2, dmask, zeta_b, xi_b, dec_b, ret_norm_w)


def _compress_kernel(rows_ref, pos_ref, w1_ref, w2_ref, out_ref):
    half = rows_ref.shape[-1]
    x = rows_ref[0, 0].astype(F32)
    lo = _dot((x + pos_ref[0, :, :half]).astype(BF16), w1_ref[0, :half, :])
    hi = _dot((x + pos_ref[0, :, half:]).astype(BF16), w1_ref[0, half:, :])
    hid = lo + pltpu.roll(hi, hi.shape[0] - 1, axis=0)
    gel = 0.5 * hid * (1.0 + jnp.tanh(np.sqrt(2.0 / np.pi) * (hid + 0.044715 * (hid * hid * hid))))
    out_ref[0, 0] = _dot(gel.astype(BF16), w2_ref[0]).astype(BF16)


def _compress(rows, pos_flat, w1, w2p):
    _, bg, ncp, half = rows.shape
    width = 2 * half
    return pl.pallas_call(
        _compress_kernel,
        grid=(2, bg),
        in_specs=[
            pl.BlockSpec((1, 1, ncp, half), lambda a, n: (a, n, 0, 0)),
            pl.BlockSpec((1, 1, width), lambda a, n: (a, 0, 0)),
            pl.BlockSpec((1, width, CMP_HIDDEN), lambda a, n: (a, 0, 0)),
            pl.BlockSpec((1, CMP_HIDDEN, LANES), lambda a, n: (a, 0, 0)),
        ],
        out_specs=pl.BlockSpec((1, 1, ncp, LANES), lambda a, n: (a, n, 0, 0)),
        out_shape=jax.ShapeDtypeStruct((2, bg, ncp, LANES), BF16),
        compiler_params=_cparams(("arbitrary", "arbitrary")),
        name="nsa_compress",
    )(rows, pos_flat, w1, w2p)


def _head_pad(q2_ref_block, h):
    x = q2_ref_block[:, (h // 2) * LANES:(h // 2 + 1) * LANES].astype(F32)
    if h % 2:
        x = pltpu.roll(x, NSA_DH, axis=1)
    lane = lax.broadcasted_iota(jnp.int32, x.shape, 1)
    return jnp.where(lane < NSA_DH, x, 0.0)


def _merge_heads(parts):
    lo = parts[0] + pltpu.roll(parts[1], NSA_DH, axis=1)
    hi = parts[2] + pltpu.roll(parts[3], NSA_DH, axis=1)
    return jnp.concatenate([lo, hi], axis=1)


def _cmp_select_kernel(q_ref, kc_ref, vc_ref, ovt_ref, ocmp_ref, bias_ref, *, tq, n_blk):
    for ti in range(q_ref.shape[0] // tq):
        _cmp_select_tile(ti * tq, q_ref, kc_ref, vc_ref, ovt_ref, ocmp_ref, bias_ref, tq, n_blk)


def _cmp_select_tile(t0, q_ref, kc_ref, vc_ref, ovt_ref, ocmp_ref, bias_ref, tq, n_blk_all):
    rows = slice(t0, t0 + tq)
    ncp = min(kc_ref.shape[2], -(-((t0 + tq) // CMP_STRIDE) // LANES) * LANES)
    n_blk = min(n_blk_all, -(-((t0 + tq) // SLC_BLOCK) // 8) * 8)
    kc = kc_ref[0, 0, :ncp, :]
    vc = vc_ref[0, 0, :ncp, :]
    ovt = ovt_ref[:n_blk, :ncp]
    scale = NSA_DH ** -0.5

    t_q = t0 + lax.broadcasted_iota(jnp.int32, (tq, ncp), 0)
    c_id = lax.broadcasted_iota(jnp.int32, (tq, ncp), 1)
    valid = (c_id * CMP_STRIDE + (CMP_BLOCK - 1)) <= t_q
    t_col = t0 + lax.broadcasted_iota(jnp.int32, (tq, 1), 0)
    any1 = jnp.where(t_col >= CMP_BLOCK - 1, 1.0, 0.0)

    imp = jnp.zeros((n_blk, tq), F32)
    o_parts = []
    for h in range(NSA_HPG):
        qp = (_head_pad(q_ref.at[rows, :], h) * scale).astype(BF16)
        s1 = jnp.where(valid, _nt_dot(qp, kc), NEG)
        m = jnp.max(s1, axis=-1, keepdims=True)
        e = jnp.exp(s1 - m)
        pb = (e * ((1.0 / jnp.sum(e, axis=-1, keepdims=True)) * any1)).astype(BF16)
        o_parts.append(_dot(pb, vc))
        imp = imp + _nt_dot(ovt, pb)
    ocmp_ref[rows, :] = _merge_heads(o_parts).astype(BF16)

    blk = lax.broadcasted_iota(jnp.int32, (n_blk, tq), 0)
    t_row = t0 + lax.broadcasted_iota(jnp.int32, (n_blk, tq), 1)
    cur = t_row // SLC_BLOCK
    causal = blk <= cur
    forced = (blk == 0) | (blk == cur) | (blk == cur - 1)
    imp = jnp.where(forced, FORCE, imp)
    imp = jnp.where(causal, imp, NEG)
    sub = 8
    groups = [imp[g * sub:(g + 1) * sub, :] for g in range(n_blk // sub)]
    ranks = [jnp.zeros((sub, tq), F32) for _ in groups]
    row_in_group = lax.broadcasted_iota(jnp.int32, (sub, tq), 0)
    for i in range(n_blk):
        r = jnp.broadcast_to(imp[i:i + 1, :], (sub, tq))
        for g, x in enumerate(groups):
            if g < i // sub:
                ahead = jnp.where(r > x, 1.0, 0.0)
            elif g > i // sub:
                ahead = jnp.where(r >= x, 1.0, 0.0)
            else:
                ahead = jnp.where(row_in_group > i % sub, jnp.where(r >= x, 1.0, 0.0), jnp.where(r > x, 1.0, 0.0))
            ranks[g] = ranks[g] + ahead
    rank = jnp.concatenate(ranks, axis=0)
    bias_t = jnp.where(rank < float(min(N_SELECT, n_blk_all)), 0.0, NEG)
    pieces = [bias_t]
    if n_blk < NSA_DH:
        pieces.append(jnp.full((NSA_DH - n_blk, tq), NEG, F32))
    pieces.append(jnp.zeros((LANES - NSA_DH, tq), F32))
    bias_ref[0, rows, :] = jnp.concatenate(pieces, axis=0).T.astype(BF16)


def _cmp_select(z, kvc, ovt, b, s, tq):
    t = b * s
    bg = b * NSA_KV_HEADS
    ncp = kvc.shape[2]
    n_blk = s // SLC_BLOCK
    qcol = OFF_NQ // 256
    return pl.pallas_call(
        functools.partial(_cmp_select_kernel, tq=tq, n_blk=n_blk),
        grid=(bg,),
        in_specs=[
            pl.BlockSpec((s, 256), lambda n: (n // NSA_KV_HEADS, qcol + n % NSA_KV_HEADS)),
            pl.BlockSpec((1, 1, ncp, LANES), lambda n: (0, n, 0, 0)),
            pl.BlockSpec((1, 1, ncp, LANES), lambda n: (1, n, 0, 0)),
            pl.BlockSpec((n_blk, ncp), lambda n: (0, 0)),
        ],
        out_specs=[
            pl.BlockSpec((s, 256), lambda n: (n // NSA_KV_HEADS, n % NSA_KV_HEADS)),
            pl.BlockSpec((1, s, LANES), lambda n: (n, 0, 0)),
        ],
        out_shape=[
            jax.ShapeDtypeStruct((t, D_MODEL), BF16),
            jax.ShapeDtypeStruct((bg, s, LANES), BF16),
        ],
        compiler_params=_cparams(("arbitrary",)),
        name="nsa_cmp_select",
    )(z, kvc, kvc, ovt)


def _normalised_heads(acc_of_head):
    outs = []
    for h in range(NSA_HPG):
        acc = acc_of_head(h)
        lane = lax.broadcasted_iota(jnp.int32, acc.shape, 1)
        denom = jnp.sum(jnp.where(lane == NSA_DH, acc, 0.0), axis=-1, keepdims=True)
        outs.append(jnp.where(lane < NSA_DH, acc * (1.0 / denom), 0.0))
    return _merge_heads(outs).astype(BF16)


def _q_pair(q_ref, h, odd_group):
    x = q_ref[:, (h // 2) * LANES:(h // 2 + 1) * LANES].astype(F32) * (NSA_DH ** -0.5)
    odd_head = jnp.bool_(h % 2 == 1)
    x = jnp.where(odd_head != odd_group, pltpu.roll(x, NSA_DH, axis=1), x)
    lane_half = lax.broadcasted_iota(jnp.int32, x.shape, 1) // NSA_DH
    return jnp.where(lane_half == odd_group.astype(jnp.int32), x, 0.0).astype(BF16)


def _flash_sel_kernel(q_ref, b_ref, k_ref, oh_ref, v_ref, o_ref, qa_scr, m_scr, acc_scr, *, tq, tk, td):
    t0 = pl.program_id(1) * tq
    odd_group = (pl.program_id(0) % 2) == 1
    for h in range(NSA_HPG):
        qa_scr[h] = jnp.concatenate([_q_pair(q_ref, h, odd_group), b_ref[0]], axis=1)
    m_scr[...] = jnp.full(m_scr.shape, NEG, F32)
    acc_scr[...] = jnp.zeros(acc_scr.shape, F32)

    def tile(start, row0, row1, width, masked):
        k = jnp.concatenate([k_ref[pl.ds(start, width), :], oh_ref[pl.ds(start, width), :]], axis=1)
        v = v_ref[pl.ds(start, width), :]
        nr = row1 - row0
        if masked:
            ok = (lax.broadcasted_iota(jnp.int32, (nr, width), 1)
                  <= lax.broadcasted_iota(jnp.int32, (nr, width), 0))
        scores = lambda h: _nt_dot(qa_scr[h, row0:row1, :], k)

        sc_next = scores(0)
        for h in range(NSA_HPG):
            sc = sc_next
            if h + 1 < NSA_HPG:
                sc_next = scores(h + 1)
            if masked:
                sc = jnp.where(ok, sc, NEG)
            m_prev = m_scr[h, row0:row1, :]
            m_new = jnp.maximum(m_prev, jnp.max(sc, axis=-1, keepdims=True))
            alpha = jnp.exp(m_prev - m_new)
            p = jnp.exp(sc - jnp.concatenate([m_new] * (width // LANES), axis=1))
            acc_scr[h, row0:row1, :] = alpha * acc_scr[h, row0:row1, :] + _dot(p.astype(BF16), v)
            m_scr[h, row0:row1, :] = m_new

    def body(kt, carry):
        tile(pl.multiple_of(kt * tk, tk), 0, tq, tk, False)
        return carry

    lax.fori_loop(0, t0 // tk, body, 0)
    for j in range(tq // td):
        tile(pl.multiple_of(t0 + j * td, td), j * td, tq, td, True)
    o_ref[...] = _normalised_heads(lambda h: acc_scr[h])


def _flash_win_kernel(q_ref, k_ref, v_ref, band_ref, o_ref, *, tq):
    t0 = pl.program_id(1) * tq
    odd_group = (pl.program_id(0) % 2) == 1
    n_tiles = WINDOW // tq + 1
    ks, vs, biases = [], [], []
    for j in range(n_tiles):
        first = t0 + (j - n_tiles + 1) * tq
        start = pl.multiple_of(jnp.maximum(first, 0), tq)
        ks.append(k_ref[pl.ds(start, tq), :])
        vs.append(v_ref[pl.ds(start, tq), :])
        biases.append(jnp.where(first < 0, NEG, band_ref[j]))

    def scores(h0):
        q = jnp.concatenate([_q_pair(q_ref, h0 + d, odd_group) for d in range(WIN_STACK)], axis=0)
        return [_nt_dot(q, ks[j]) + jnp.concatenate([biases[j]] * WIN_STACK, axis=0) for j in range(n_tiles)]

    accs = []
    scs_next = scores(0)
    for h0 in range(0, NSA_HPG, WIN_STACK):
        scs = scs_next
        if h0 + WIN_STACK < NSA_HPG:
            scs_next = scores(h0 + WIN_STACK)
        m = jnp.max(scs[0], axis=-1, keepdims=True)
        for sc in scs[1:]:
            m = jnp.maximum(m, jnp.max(sc, axis=-1, keepdims=True))
        acc = _dot(jnp.exp(scs[0] - m).astype(BF16), vs[0])
        for j in range(1, n_tiles):
            acc = acc + _dot(jnp.exp(scs[j] - m).astype(BF16), vs[j])
        accs.extend(acc[d * tq:(d + 1) * tq] for d in range(WIN_STACK))

    o_ref[...] = _normalised_heads(lambda h: accs[h])


def _flash(z, sel_bias, blk_onehot, b, s, tq, tk, td, mode):
    t = b * s
    nq = s // tq
    bg = b * NSA_KV_HEADS
    grp = lambda n: n % NSA_KV_HEADS
    bat = lambda n: n // NSA_KV_HEADS
    off_k, off_v = (OFF_KS, OFF_VS) if mode == "sel" else (OFF_KW, OFF_VW)
    q_spec = pl.BlockSpec((tq, 256), lambda n, i: (bat(n) * nq + i, OFF_NQ // 256 + grp(n)))
    k_spec = pl.BlockSpec((s, LANES), lambda n, i: (bat(n), off_k // LANES + grp(n) // 2))
    v_spec = pl.BlockSpec((s, LANES), lambda n, i: (bat(n), off_v // LANES + grp(n)))
    if mode == "sel":
        body = functools.partial(_flash_sel_kernel, tq=tq, tk=tk, td=td)
        in_specs = [q_spec, pl.BlockSpec((1, tq, LANES), lambda n, i: (n, i, 0)), k_spec,
                    pl.BlockSpec((s, LANES), lambda n, i: (0, 0)), v_spec]
        args = (z, sel_bias, z, blk_onehot, z)
        scratch = [pltpu.VMEM((NSA_HPG, tq, 2 * LANES), BF16), pltpu.VMEM((NSA_HPG, tq, LANES), F32),
                   pltpu.VMEM((NSA_HPG, tq, LANES), F32)]
    else:
        body = functools.partial(_flash_win_kernel, tq=tq)
        n_tiles = WINDOW // tq + 1
        rel = (np.arange(tq)[None, :, None] - np.arange(tq)[None, None, :]
               + (n_tiles - 1 - np.arange(n_tiles))[:, None, None] * tq)
        band = jnp.asarray(np.where((rel >= 0) & (rel < WINDOW), 0.0, NEG), F32)
        in_specs = [q_spec, k_spec, v_spec, pl.BlockSpec((n_tiles, tq, tq), lambda n, i: (0, 0, 0))]
        args = (z, z, z, band)
        scratch = []
    return pl.pallas_call(
        body,
        grid=(bg, nq),
        in_specs=in_specs,
        out_specs=pl.BlockSpec((tq, 256), lambda n, i: (bat(n) * nq + i, grp(n))),
        out_shape=jax.ShapeDtypeStruct((t, D_MODEL), BF16),
        scratch_shapes=scratch,
        compiler_params=_cparams(("arbitrary", "arbitrary")),
        name="nsa_flash_" + mode,
    )(*args)


def _shifted(u, prev, n):
    rolled = pltpu.roll(u, n, axis=0)
    head = rolled[:HALO_ROWS]
    row = lax.broadcasted_iota(jnp.int32, head.shape, 0)
    for r in range(n):
        head = jnp.where(row == r, prev[HALO_ROWS - n + r:HALO_ROWS - n + r + 1, :], head)
    return jnp.concatenate([head, rolled[HALO_ROWS:]], axis=0)


def _causal_conv3(u, prev, w):
    return w[0:1, :] * _shifted(u, prev, 2) + w[1:2, :] * _shifted(u, prev, 1) + w[2:3, :] * u


def _merge_kernel(x_ref, yret_ref, sb_ref, sc_ref, sx_ref, sch_ref, sxh_ref, mg0_ref, mg1_ref, mg2_ref,
                  ng_ref, ocmp_ref, oslc_ref, owin_ref, cw_ref, e_ref, wret_ref, wsc_ref, wnsa_ref,
                  wmix_ref, out_ref, *, tiles_per_seq):
    first = (pl.program_id(0) % tiles_per_seq) == 0
    u = sc_ref[...].astype(F32) * sx_ref[...].astype(F32)
    prev = sch_ref[...].astype(F32) * sxh_ref[...].astype(F32)
    prev = jnp.where(first, 0.0, prev)
    y_sc = (sb_ref[...].astype(F32) * _causal_conv3(u, prev, cw_ref[...])).astype(BF16)

    gates = _dot(_sigmoid(ng_ref[...].astype(F32)).astype(BF16), e_ref[...])
    y_nsa = (gates[:, 0:D_MODEL] * ocmp_ref[...].astype(F32)
             + gates[:, D_MODEL:2 * D_MODEL] * oslc_ref[...].astype(F32)
             + gates[:, 2 * D_MODEL:3 * D_MODEL] * owin_ref[...].astype(F32)).astype(BF16)

    merged = (_sigmoid(mg0_ref[...].astype(F32)) * _dot(yret_ref[...], wret_ref[...])
              + _sigmoid(mg1_ref[...].astype(F32)) * _dot(y_sc, wsc_ref[...])
              + _sigmoid(mg2_ref[...].astype(F32)) * _dot(y_nsa, wnsa_ref[...]))
    out_ref[...] = x_ref[...] + _dot(merged.astype(BF16), wmix_ref[...])


def _merge(x2, z, y_ret, o_cmp, o_slc, o_win, conv_w, gate_expand, w_ret, w_sc, w_nsa, w_mix, layer, s, tm):
    t = x2.shape[0]
    row = lambda c: (lambda i: (i, c))
    halo = lambda c: (lambda i: (jnp.maximum(i * (tm // HALO_ROWS) - 1, 0), c))
    const = lambda i: (0, 0)
    wspec = pl.BlockSpec((None, D_MODEL, D_MODEL), lambda i: (layer, 0, 0),
                         pipeline_mode=pl.Buffered(1))
    return pl.pallas_call(
        functools.partial(_merge_kernel, tiles_per_seq=s // tm),
        grid=(t // tm,),
        in_specs=[
            pl.BlockSpec((tm, D_MODEL), row(0)),
            pl.BlockSpec((tm, D_MODEL), row(0)),
            pl.BlockSpec((tm, D_MODEL), row(OFF_SB // D_MODEL)),
            pl.BlockSpec((tm, D_MODEL), row(OFF_SC // D_MODEL)),
            pl.BlockSpec((tm, D_MODEL), row(OFF_SX // D_MODEL)),
            pl.BlockSpec((HALO_ROWS, D_MODEL), halo(OFF_SC // D_MODEL)),
            pl.BlockSpec((HALO_ROWS, D_MODEL), halo(OFF_SX // D_MODEL)),
            pl.BlockSpec((tm, D_MODEL), row(OFF_MG // D_MODEL)),
            pl.BlockSpec((tm, D_MODEL), row(OFF_MG // D_MODEL + 1)),
            pl.BlockSpec((tm, D_MODEL), row(OFF_MG // D_MODEL + 2)),
            pl.BlockSpec((tm, LANES), row(OFF_NG // LANES)),
            pl.BlockSpec((tm, D_MODEL), row(0)),
            pl.BlockSpec((tm, D_MODEL), row(0)),
            pl.BlockSpec((tm, D_MODEL), row(0)),
            pl.BlockSpec((CONV_WIDTH, D_MODEL), const),
            pl.BlockSpec((LANES, 3 * D_MODEL), const),
            wspec, wspec, wspec, wspec,
        ],
        out_specs=pl.BlockSpec((tm, D_MODEL), row(0)),
        out_shape=jax.ShapeDtypeStruct((t, D_MODEL), F32),
        compiler_params=_cparams(("arbitrary",)),
        name="merge",
    )(x2, y_ret, z, z, z, z, z, z, z, z, z, o_cmp, o_slc, o_win, conv_w, gate_expand,
      w_ret, w_sc, w_nsa, w_mix)


def _ffn_kernel(x_ref, nw_ref, wup_ref, cw_ref, wd_ref, fw_ref, out_ref, h_scr, ca_scr, cv_scr,
                *, tiles_per_seq, final_norm, tf):
    @pl.when((pl.program_id(0) % tiles_per_seq) == 0)
    def _():
        ca_scr[...] = jnp.zeros_like(ca_scr)
        cv_scr[...] = jnp.zeros_like(cv_scr)

    h_scr[...] = _rms(x_ref[...], nw_ref[...]).astype(BF16)
    out_ref[...] = x_ref[...]
    tm = h_scr.shape[0]
    nf = D_FF // tf
    cols_a = lambda j: slice(j * tf, (j + 1) * tf)
    cols_v = lambda j: slice(D_FF + j * tf, D_FF + (j + 1) * tf)

    def up(j):
        h = h_scr[...]
        return _dot(h, wup_ref[:, cols_a(j)]), _dot(h, wup_ref[:, cols_v(j)])

    def gate(j, ua, uv):
        a = _causal_conv3(ua, ca_scr[j], cw_ref[:, cols_a(j)])
        v = _causal_conv3(uv, cv_scr[j], cw_ref[:, cols_v(j)])
        ca_scr[j] = ua[tm - HALO_ROWS:, :]
        cv_scr[j] = uv[tm - HALO_ROWS:, :]
        return (a * _sigmoid(a) * v).astype(BF16)

    def down(first_tile, acts):
        rows = slice(first_tile * tf, (first_tile + len(acts)) * tf)
        out_ref[...] += _dot(jnp.concatenate(acts, axis=1), wd_ref[rows, :])

    u_next = up(0)
    pending = []
    for j in range(nf):
        u_cur = u_next
        if j + 1 < nf:
            u_next = up(j + 1)
        act = gate(j, *u_cur)
        if len(pending) == FFN_DOWN_GROUP:
            down(j - len(pending), pending)
            pending = []
        pending.append(act)
    down(nf - len(pending), pending)
    if final_norm:
        out_ref[...] = _rms(out_ref[...], fw_ref[...])


def _ffn(x2, norm_w, w_up, conv_w, w_down, final_w, layer, s, tm, tf, final_norm):
    t = x2.shape[0]
    nf = D_FF // tf
    const = lambda i: (0, 0)
    of_layer = lambda i: (layer, 0, 0)
    resident = dict(index_map=of_layer, pipeline_mode=pl.Buffered(1))
    return pl.pallas_call(
        functools.partial(_ffn_kernel, tiles_per_seq=s // tm, final_norm=final_norm, tf=tf),
        grid=(t // tm,),
        in_specs=[
            pl.BlockSpec((tm, D_MODEL), lambda i: (i, 0)),
            pl.BlockSpec((1, D_MODEL), const),
            pl.BlockSpec((None, D_MODEL, 2 * D_FF), **resident),
            pl.BlockSpec((None, CONV_WIDTH, 2 * D_FF), of_layer),
            pl.BlockSpec((None, D_FF, D_MODEL), **resident),
            pl.BlockSpec((1, D_MODEL), const),
        ],
        out_specs=pl.BlockSpec((tm, D_MODEL), lambda i: (i, 0)),
        out_shape=jax.ShapeDtypeStruct((t, D_MODEL), F32),
        scratch_shapes=[
            pltpu.VMEM((tm, D_MODEL), BF16),
            pltpu.VMEM((nf, HALO_ROWS, tf), F32),
            pltpu.VMEM((nf, HALO_ROWS, tf), F32),
        ],
        compiler_params=_cparams(("arbitrary",)),
        name="ffn",
    )(x2, norm_w, w_up, conv_w, w_down, final_w)


def _prep_w_in(w_in_l):
    w_in_l = w_in_l.astype(BF16)
    gw = NSA_KV_HEADS * NSA_DH
    src = OFF_MG
    k_c, v_c, k_s, v_s, k_w, v_w = [w_in_l[:, src + n * gw: src + (n + 1) * gw] for n in range(6)]
    ng = w_in_l[:, src + 6 * gw: src + 6 * gw + 3 * NSA_HEADS]
    ng = ng.reshape(D_MODEL, NSA_HEADS, 3).transpose(0, 2, 1).reshape(D_MODEL, 3 * NSA_HEADS)
    mg = w_in_l[:, src + 6 * gw + 3 * NSA_HEADS:]

    def lane_pad(v):
        v = v.reshape(D_MODEL, NSA_KV_HEADS, NSA_DH)
        return jnp.pad(v, ((0, 0), (0, 0), (0, LANES - NSA_DH))).reshape(D_MODEL, NSA_KV_HEADS * LANES)

    pad = jnp.zeros((D_MODEL, Z_WIDTH - OFF_NG - 3 * NSA_HEADS), w_in_l.dtype)
    return jnp.concatenate([w_in_l[:, :OFF_MG], mg, k_c, v_c, k_s, k_w, lane_pad(v_s), lane_pad(v_w), ng, pad],
                           axis=1).astype(BF16)


def _z_bias():
    bias = np.zeros((1, Z_WIDTH), np.float32)
    for off in (OFF_VS, OFF_VW):
        for g in range(NSA_KV_HEADS):
            bias[0, off + g * LANES + NSA_DH] = 1.0
    return jnp.asarray(bias)


def _gate_expander():
    e = np.zeros((LANES, 3 * D_MODEL), np.float32)
    for br in range(3):
        for h in range(NSA_HEADS):
            e[br * NSA_HEADS + h, br * D_MODEL + h * NSA_DH: br * D_MODEL + (h + 1) * NSA_DH] = 1.0
    return jnp.asarray(e, BF16)


def _overlap_t(s, ncp):
    nc = (s - CMP_BLOCK) // CMP_STRIDE + 1
    ns = s // SLC_BLOCK
    ci = np.arange(ncp) * CMP_STRIDE
    sj = np.arange(ns) * SLC_BLOCK
    ov = ((ci[None, :] < sj[:, None] + SLC_BLOCK) & (ci[None, :] + CMP_BLOCK > sj[:, None])
          & (np.arange(ncp)[None, :] < nc))
    return jnp.asarray(ov.astype(np.float32), BF16)


def kernel(x, attn_norm_w, w_in, ret_norm_w, w_ret_out, sc_conv_w, w_sc_out, nsa_cmp_pos, nsa_cmp_w1,
           nsa_cmp_w2, w_nsa_out, w_mix_out, ffn_norm_w, w_ffn_up, ffn_conv_w, w_ffn_down, final_norm_w):
    b, s, _ = x.shape
    t = b * s
    depth = w_in.shape[0]
    bg = b * NSA_KV_HEADS
    ncp = s // CMP_STRIDE
    assert s % 256 == 0 and s // SLC_BLOCK <= NSA_DH

    tm_in = min(1024, t)
    tm_merge = 512
    tm_ffn = min(1024, s)
    tq_cmp = 512
    tq_sel = 1024
    tk_sel = 512
    td_sel = 512
    tq_win = 512
    tr = 512

    ret_consts = _retention_consts(s)
    gate_expand = _gate_expander()
    z_bias = _z_bias()
    ovt = _overlap_t(s, ncp)
    onehot = np.zeros((s, LANES), np.float32)
    onehot[np.arange(s), (np.arange(s) // SLC_BLOCK) % NSA_DH] = 1.0
    blk_onehot = jnp.asarray(onehot, BF16)
    w_ret_bf, w_sc_bf, w_nsa_bf, w_mix_bf, w_up_bf, w_down_bf = (
        w.astype(BF16) for w in (w_ret_out, w_sc_out, w_nsa_out, w_mix_out, w_ffn_up, w_ffn_down))

    x2 = x.reshape(t, D_MODEL)
    for l in range(depth):
        z = _inproj(x2, attn_norm_w[l][None], _prep_w_in(w_in[l]), z_bias, tm_in, Z_TILE)
        y_ret = _retention(z, ret_norm_w[l][None], ret_consts, b, s, tr)

        kv_c = z[:, OFF_KC:OFF_KC + 2 * NSA_KV_HEADS * NSA_DH]
        kv_c = kv_c.reshape(b, ncp, CMP_STRIDE, 2, NSA_KV_HEADS, NSA_DH).transpose(3, 0, 4, 1, 2, 5)
        kv_c = kv_c.reshape(2, bg, ncp, CMP_STRIDE * NSA_DH)
        pos_flat = nsa_cmp_pos[l].reshape(2, 1, CMP_BLOCK * NSA_DH)
        w2p = jnp.pad(nsa_cmp_w2[l], ((0, 0), (0, 0), (0, LANES - NSA_DH))).astype(BF16)
        kvc = _compress(kv_c, pos_flat, nsa_cmp_w1[l].astype(BF16), w2p)

        o_cmp, sel_bias = _cmp_select(z, kvc, ovt, b, s, tq_cmp)
        o_slc = _flash(z, sel_bias, blk_onehot, b, s, tq_sel, tk_sel, td_sel, "sel")
        o_win = _flash(z, None, None, b, s, tq_win, None, None, "win")

        x2 = _merge(x2, z, y_ret, o_cmp, o_slc, o_win, sc_conv_w[l], gate_expand,
                    w_ret_bf, w_sc_bf, w_nsa_bf, w_mix_bf, l, s, tm_merge)
        x2 = _ffn(x2, ffn_norm_w[l][None], w_up_bf, ffn_conv_w, w_down_bf, final_norm_w[None],
                  l, s, tm_ffn, 256, l == depth - 1)
    return x2.reshape(b, s, D_MODEL)
```
